```python
import math
import jax
import jax.numpy as jnp
from jax import lax
import numpy as np

D_MODEL = 1024
BATCH = 32
SEQ = 2048
DEPTH = 4

GRID_W = 64
CTX_LEN = 256
N_DIR = 2
D_LRU = D_MODEL
LRU_HEADS = 16
LRU_HEAD_DIM = D_LRU // LRU_HEADS
LRU_CONV = 4
LRU_C = 8.0
D_S5 = D_MODEL // 2
S5_GROUP = 16
S5_GROUPS = D_S5 // S5_GROUP
S5_STATE = 64
D_FF = 3 * D_MODEL
FFN_CONV = 3
N_MOD = 6
DEEPNORM_ALPHA = (2.0 * DEPTH) ** 0.25
DEEPNORM_BETA = (8.0 * DEPTH) ** -0.25
LN_EPS = 1e-5
F32 = jnp.float32

kernel_name = 'hybrid_rglru_s5_convffn_diffusion_trunk'


def layer_norm(x, g, b):
    xf = x.astype(F32)
    mu = jnp.mean(xf, axis=-1, keepdims=True)
    var = jnp.mean(jnp.square(xf - mu), axis=-1, keepdims=True)
    return ((xf - mu) * lax.rsqrt(var + LN_EPS) * g.astype(F32) + b.astype(F32)).astype(x.dtype)


def directional_conv(x, w, b, reverse):
    k_w, length = w.shape[0], x.shape[1]
    pad = ((0, 0), (0, k_w - 1), (0, 0)) if reverse else ((0, 0), (k_w - 1, 0), (0, 0))
    xp = jnp.pad(x, pad)
    y = b
    for k in range(k_w):
        start = (k_w - 1 - k) if reverse else k
        y = y + w[k] * xp[:, start:start + length]
    return y


def _real_combine(e1, e2):
    a1, b1 = e1
    a2, b2 = e2
    return a1 * a2, a2 * b1 + b2


def real_linear_scan(a, b, h0, reverse):
    a_cum, h = lax.associative_scan(_real_combine, (a, b), reverse=reverse, axis=1)
    if h0 is not None:
        h = h + a_cum * h0[:, None]
    h_last = h[:, 0] if reverse else h[:, -1]
    return h, h_last


def _cplx_combine(e1, e2):
    ar1, ai1, br1, bi1 = e1
    ar2, ai2, br2, bi2 = e2
    return (ar1 * ar2 - ai1 * ai2, ar1 * ai2 + ai1 * ar2,
            ar2 * br1 - ai2 * bi1 + br2, ar2 * bi1 + ai2 * br1 + bi2)


def block_diag_linear(x, w, b):
    bsz, length, _ = x.shape
    xh = x.reshape(bsz, length, LRU_HEADS, LRU_HEAD_DIM)
    y = jnp.einsum('blhi,hij->blhj', xh, w.astype(F32)).reshape(bsz, length, D_LRU)
    return y + b.astype(F32)


def rglru_coeffs(xc, wa, ba, wx, bx, lam):
    r = jax.nn.sigmoid(block_diag_linear(xc, wa, ba))
    i = jax.nn.sigmoid(block_diag_linear(xc, wx, bx))
    log_a = -LRU_C * r * jax.nn.softplus(-lam.astype(F32))
    a = jnp.exp(log_a)
    gated_x = jnp.sqrt(-jnp.expm1(2.0 * log_a)) * (i * xc)
    return a, gated_x


def rglru_branch(xa_c, xa_l, conv_w, conv_b, wa, ba, wx, bx, lam, ctx_out):
    dtype = xa_l.dtype
    xa_c = xa_c.astype(F32)
    xa_l = xa_l.astype(F32)
    ys_c, ys_l = [], []
    for d in range(N_DIR):
        rev = d == 1
        cw, cb = conv_w[d].astype(F32), conv_b[d].astype(F32)
        a_c, gx_c = rglru_coeffs(directional_conv(xa_c, cw, cb, rev), wa[d], ba[d], wx[d], bx[d], lam[d])
        a_l, gx_l = rglru_coeffs(directional_conv(xa_l, cw, cb, rev), wa[d], ba[d], wx[d], bx[d], lam[d])
        h_c, h_c_last = real_linear_scan(a_c, gx_c, None, rev)
        h_l, _ = real_linear_scan(a_l, gx_l, h_c_last, rev)
        ys_c.append(h_c)
        ys_l.append(h_l)
    y_l = (ys_l[0] + ys_l[1]).astype(dtype)
    y_c = (ys_c[0] + ys_c[1]).astype(dtype) if ctx_out else None
    return y_c, y_l


def s5_discretise(lam_re, lam_im, log_dt, b_re, b_im):
    dt = jnp.exp(log_dt.astype(F32))[:, None]
    lr, li = lam_re.astype(F32), lam_im.astype(F32)
    mag = jnp.exp(lr * dt)
    ar = mag * jnp.cos(li * dt)
    ai = mag * jnp.sin(li * dt)
    den = lr * lr + li * li
    fr = ((ar - 1.0) * lr + ai * li) / den
    fi = (ai * lr - (ar - 1.0) * li) / den
    br, bi = b_re.astype(F32), b_im.astype(F32)
    bbr = fr[..., None] * br - fi[..., None] * bi
    bbi = fr[..., None] * bi + fi[..., None] * br
    return ar, ai, bbr, bbi


def s5_scan(u, ar, ai, bbr, bbi, h0, reverse):
    br = jnp.einsum('blgc,gpc->blgp', u, bbr)
    bi = jnp.einsum('blgc,gpc->blgp', u, bbi)
    arb = jnp.broadcast_to(ar, br.shape)
    aib = jnp.broadcast_to(ai, br.shape)
    a_re, a_im, hr, hi = lax.associative_scan(_cplx_combine, (arb, aib, br, bi), reverse=reverse, axis=1)
    if h0 is not None:
        h0r, h0i = h0[0][:, None], h0[1][:, None]
        hr = hr + a_re * h0r - a_im * h0i
        hi = hi + a_re * h0i + a_im * h0r
    idx = 0 if reverse else -1
    return hr, hi, (hr[:, idx], hi[:, idx])


def s5_readout(hr, hi, c_re, c_im):
    return (jnp.einsum('blgp,gcp->blgc', hr, c_re.astype(F32))
            - jnp.einsum('blgp,gcp->blgc', hi, c_im.astype(F32)))


def s5_glu(y, u, d_skip, glu_w, glu_b, dtype):
    bsz, length = y.shape[:2]
    y = (y + d_skip.astype(F32).reshape(S5_GROUPS, S5_GROUP) * u).reshape(bsz, length, D_S5).astype(dtype)
    z = jax.nn.gelu(y)
    return z * jax.nn.sigmoid(z @ glu_w + glu_b)


def s5_branch(xb_c, xb_l, lam_re, lam_im, log_dt, b_re, b_im, c_re, c_im, d_skip, glu_w, glu_b, ctx_out):
    dtype = xb_l.dtype
    u_c = xb_c.astype(F32).reshape(xb_c.shape[0], xb_c.shape[1], S5_GROUPS, S5_GROUP)
    u_l = xb_l.astype(F32).reshape(xb_l.shape[0], xb_l.shape[1], S5_GROUPS, S5_GROUP)
    ys_c, ys_l = [], []
    for d in range(N_DIR):
        rev = d == 1
        ar, ai, bbr, bbi = s5_discretise(lam_re[d], lam_im[d], log_dt[d], b_re[d], b_im[d])
        hr_c, hi_c, h_c_last = s5_scan(u_c, ar, ai, bbr, bbi, None, rev)
        hr_l, hi_l, _ = s5_scan(u_l, ar, ai, bbr, bbi, h_c_last, rev)
        ys_l.append(s5_readout(hr_l, hi_l, c_re[d], c_im[d]))
        if ctx_out:
            ys_c.append(s5_readout(hr_c, hi_c, c_re[d], c_im[d]))
    y_l = s5_glu(ys_l[0] + ys_l[1], u_l, d_skip, glu_w, glu_b, dtype)
    y_c = s5_glu(ys_c[0] + ys_c[1], u_c, d_skip, glu_w, glu_b, dtype) if ctx_out else None
    return y_c, y_l


def merge_branches(g_a, g_b, y_a, y_b, p_a, p_b, w_out):
    m = jax.nn.sigmoid(g_a) * (y_a @ p_a) + jax.nn.sigmoid(g_b) * (y_b @ p_b)
    return m @ w_out


def conv_ffn(u, rows, cols, w_up, conv_w, conv_b, w_down):
    bsz = u.shape[0]
    gate, val = jnp.split(u @ w_up, 2, axis=-1)
    g = gate.reshape(bsz, rows, cols, D_FF)
    g = lax.conv_general_dilated(g, conv_w[:, :, None, :].astype(g.dtype), (1, 1), 'SAME',
                                 dimension_numbers=('NHWC', 'HWIO', 'NHWC'), feature_group_count=D_FF)
    g = g.reshape(bsz, rows * cols, D_FF) + conv_b
    return (jax.nn.gelu(g) * val) @ w_down


def setup_inputs(seed: int = 0) -> dict:
    key = jax.random.key(seed)
    ks = iter(jax.random.split(key, 48))

    def nrm(shape, scale):
        return scale * jax.random.normal(next(ks), shape, F32)

    L, D = DEPTH, D_MODEL
    G, P, S = S5_GROUPS, S5_STATE, S5_GROUP
    a_pow = jax.random.uniform(next(ks), (L, N_DIR, D_LRU), F32, 0.9, 0.999)
    a_base = a_pow ** (1.0 / LRU_C)
    lru_lam = jnp.log(a_base) - jnp.log1p(-a_base)
    s5_lam_re = -0.5 + nrm((L, N_DIR, G, P), 0.01)
    s5_lam_im = jnp.pi * jnp.arange(P, dtype=F32) + nrm((L, N_DIR, G, P), 0.01)
    s5_log_dt = jax.random.uniform(next(ks), (L, N_DIR, G), F32, math.log(1e-3), math.log(1e-1))
    return {
        'x': nrm((BATCH, SEQ, D), 1.0),
        'c': nrm((BATCH, D), 1.0),
        'ctx': nrm((BATCH, CTX_LEN, D), 1.0),
        'c_ctx': nrm((D,), 1.0),
        'ada_w': nrm((L, D, N_MOD * D), 0.1 * D ** -0.5),
        'ada_b': nrm((L, N_MOD * D), 0.01),
        'w_in': nrm((L, D, D_LRU + D_S5 + 2 * D), D ** -0.5),
        'lru_conv_w': nrm((L, N_DIR, LRU_CONV, D_LRU), LRU_CONV ** -0.5),
        'lru_conv_b': nrm((L, N_DIR, D_LRU), 0.01),
        'lru_wa': nrm((L, N_DIR, LRU_HEADS, LRU_HEAD_DIM, LRU_HEAD_DIM), LRU_HEAD_DIM ** -0.5),
        'lru_ba': nrm((L, N_DIR, D_LRU), 0.01),
        'lru_wx': nrm((L, N_DIR, LRU_HEADS, LRU_HEAD_DIM, LRU_HEAD_DIM), LRU_HEAD_DIM ** -0.5),
        'lru_bx': nrm((L, N_DIR, D_LRU), 0.01),
        'lru_lam': lru_lam,
        's5_lam_re': s5_lam_re,
        's5_lam_im': s5_lam_im,
        's5_log_dt': s5_log_dt,
        's5_b_re': nrm((L, N_DIR, G, P, S), (2.0 * S) ** -0.5),
        's5_b_im': nrm((L, N_DIR, G, P, S), (2.0 * S) ** -0.5),
        's5_c_re': nrm((L, N_DIR, G, S, P), P ** -0.5),
        's5_c_im': nrm((L, N_DIR, G, S, P), P ** -0.5),
        's5_d': nrm((L, D_S5), 1.0),
        's5_glu_w': nrm((L, D_S5, D_S5), D_S5 ** -0.5),
        's5_glu_b': nrm((L, D_S5), 0.01),
        'p_a': nrm((L, D_LRU, D), D_LRU ** -0.5),
        'p_b': nrm((L, D_S5, D), D_S5 ** -0.5),
        'w_out': nrm((L, D, D), DEEPNORM_BETA * D ** -0.5),
        'ln1_g': 1.0 + nrm((L, D), 0.01),
        'ln1_b': nrm((L, D), 0.01),
        'ffn_up': nrm((L, D, 2 * D_FF), D ** -0.5),
        'ffn_conv_w': nrm((L, FFN_CONV, FFN_CONV, D_FF), 1.0 / FFN_CONV),
        'ffn_conv_b': nrm((L, D_FF), 0.01),
        'ffn_down': nrm((L, D_FF, D), DEEPNORM_BETA * D_FF ** -0.5),
        'ln2_g': 1.0 + nrm((L, D), 0.01),
        'ln2_b': nrm((L, D), 0.01),
    }


def reference(x, c, ctx, c_ctx, ada_w, ada_b, w_in, lru_conv_w, lru_conv_b, lru_wa, lru_ba, lru_wx, lru_bx,
              lru_lam, s5_lam_re, s5_lam_im, s5_log_dt, s5_b_re, s5_b_im, s5_c_re, s5_c_im, s5_d, s5_glu_w,
              s5_glu_b, p_a, p_b, w_out, ln1_g, ln1_b, ffn_up, ffn_conv_w, ffn_conv_b, ffn_down, ln2_g, ln2_b):
    rows = x.shape[1] // GRID_W
    n_ctx = ctx.shape[1]
    splits = (D_LRU, D_LRU + D_S5, D_LRU + D_S5 + D_MODEL)
    silu_c = jax.nn.silu(c)
    silu_cc = jax.nn.silu(c_ctx)
    for l in range(DEPTH):
        ctx_out = l < DEPTH - 1
        sh1_l, sc1_l, g1_l, sh2_l, sc2_l, g2_l = jnp.split(
            (silu_c @ ada_w[l] + ada_b[l])[:, None, :], N_MOD, axis=-1)
        sh1_c, sc1_c, g1_c, sh2_c, sc2_c, g2_c = jnp.split(
            (silu_cc @ ada_w[l] + ada_b[l])[None, None, :], N_MOD, axis=-1)

        xa_l, xb_l, ga_l, gb_l = jnp.split((x * (1.0 + sc1_l) + sh1_l) @ w_in[l], splits, axis=-1)
        xa_c, xb_c, ga_c, gb_c = jnp.split((ctx * (1.0 + sc1_c) + sh1_c) @ w_in[l], splits, axis=-1)

        ya_c, ya_l = rglru_branch(xa_c, xa_l, lru_conv_w[l], lru_conv_b[l], lru_wa[l], lru_ba[l],
                                  lru_wx[l], lru_bx[l], lru_lam[l], ctx_out)
        yb_c, yb_l = s5_branch(xb_c, xb_l, s5_lam_re[l], s5_lam_im[l], s5_log_dt[l], s5_b_re[l], s5_b_im[l],
                               s5_c_re[l], s5_c_im[l], s5_d[l], s5_glu_w[l], s5_glu_b[l], ctx_out)

        out_l = merge_branches(ga_l, gb_l, ya_l, yb_l, p_a[l], p_b[l], w_out[l])
        x = layer_norm(DEEPNORM_ALPHA * x + (1.0 + g1_l) * out_l, ln1_g[l], ln1_b[l])
        f_l = conv_ffn(x * (1.0 + sc2_l) + sh2_l, rows, GRID_W, ffn_up[l], ffn_conv_w[l], ffn_conv_b[l], ffn_down[l])
        x = layer_norm(DEEPNORM_ALPHA * x + (1.0 + g2_l) * f_l, ln2_g[l], ln2_b[l])

        if ctx_out:
            out_c = merge_branches(ga_c, gb_c, ya_c, yb_c, p_a[l], p_b[l], w_out[l])
            ctx = layer_norm(DEEPNORM_ALPHA * ctx + (1.0 + g1_c) * out_c, ln1_g[l], ln1_b[l])
            f_c = conv_ffn(ctx * (1.0 + sc2_c) + sh2_c, 1, n_ctx, ffn_up[l], ffn_conv_w[l], ffn_conv_b[l], ffn_down[l])
            ctx = layer_norm(DEEPNORM_ALPHA * ctx + (1.0 + g2_c) * f_c, ln2_g[l], ln2_b[l])
    return x
```

```python
import functools
import math

import jax
import jax.numpy as jnp
from jax import lax
from jax.experimental import pallas as pl
from jax.experimental.pallas import tpu as pltpu

F32 = jnp.float32
BF16 = jnp.bfloat16

GRID_W = 64
LRU_C = 8.0
LN_EPS = 1e-5
S5_GROUP = 16
MXU_TILE = 256
VMEM_LIMIT = 60 * 1024 * 1024

TB_IN = 16
TB_MIX = 16
FFN_ROWS = 4
FFN_BB = 8
FFN_CK = 256
LRU_CHUNK = 512
S5_CHUNK = 256


def _cparams(sem):
    return pltpu.CompilerParams(dimension_semantics=sem, vmem_limit_bytes=VMEM_LIMIT)


def _const_spec(shape):
    nd = len(shape)
    return pl.BlockSpec(shape, lambda *_: (0,) * nd, pipeline_mode=pl.Buffered(1))


def _gelu_tanh(x):
    return 0.5 * x * (1.0 + jnp.tanh(math.sqrt(2.0 / math.pi) * (x + 0.044715 * (x * x * x))))


def _sigmoid(x):
    return 1.0 / (1.0 + jnp.exp(-x))


def _layer_norm(v, g, b):
    mu = jnp.mean(v, axis=-1, keepdims=True)
    xc = v - mu
    var = jnp.mean(xc * xc, axis=-1, keepdims=True)
    return xc * lax.rsqrt(var + LN_EPS) * g + b


def _mod_kernel(c_ref, w_ref, b_ref, o_ref):
    c = c_ref[...]
    s = (c * _sigmoid(c)).astype(BF16)
    o_ref[0] = jnp.dot(s, w_ref[0].astype(BF16), preferred_element_type=F32) + b_ref[0]


def _modulation(cond, ada_w, ada_b):
    depth, d, n = ada_w.shape
    m = cond.shape[0]
    tn = min(n, 1536)
    return pl.pallas_call(
        _mod_kernel,
        grid=(depth, n // tn),
        in_specs=[pl.BlockSpec((m, d), lambda l, j: (0, 0)),
                  pl.BlockSpec((1, d, tn), lambda l, j: (l, 0, j)),
                  pl.BlockSpec((1, 1, tn), lambda l, j: (l, 0, j))],
        out_specs=pl.BlockSpec((1, m, tn), lambda l, j: (l, 0, j)),
        out_shape=jax.ShapeDtypeStruct((depth, m, n), F32),
        compiler_params=_cparams(("arbitrary", "arbitrary")),
        name="adaln_mod",
    )(cond, ada_w, ada_b.reshape(depth, 1, n))


def _s5_disc_kernel(lr_ref, li_ref, ldt_ref, br_ref, bi_ref, ar_ref, ai_ref, bbr_ref, bbi_ref):
    lr, li = lr_ref[...], li_ref[...]
    dt = jnp.exp(ldt_ref[...])
    mag = jnp.exp(lr * dt)
    ar = mag * jnp.cos(li * dt)
    ai = mag * jnp.sin(li * dt)
    den = lr * lr + li * li
    fr = ((ar - 1.0) * lr + ai * li) / den
    fi = (ai * lr - (ar - 1.0) * li) / den
    br, bi = br_ref[...], bi_ref[...]
    ar_ref[...] = ar
    ai_ref[...] = ai
    bbr_ref[...] = fr * br - fi * bi
    bbi_ref[...] = fr * bi + fi * br


def _s5_discretise(lam_re, lam_im, log_dt, b_re, b_im):
    lead = b_re.shape[:-2]
    p, s = b_re.shape[-2:]
    n = math.prod(lead)
    full = lead + (p, s)
    flat = lambda a: jnp.broadcast_to(a, full).reshape(n, p * s)
    args = (flat(lam_re[..., None]), flat(lam_im[..., None]), flat(log_dt[..., None, None]), flat(b_re), flat(b_im))
    rb = 8 if n % 8 == 0 else n
    spec = pl.BlockSpec((rb, p * s), lambda i: (i, 0))
    outs = pl.pallas_call(
        _s5_disc_kernel,
        grid=(n // rb,),
        in_specs=[spec] * 5,
        out_specs=[spec] * 4,
        out_shape=[jax.ShapeDtypeStruct((n, p * s), F32)] * 4,
        compiler_params=_cparams(("arbitrary",)),
        name="s5_discretise",
    )(*args)
    ar, ai, bbr, bbi = (o.reshape(full) for o in outs)
    return ar[..., 0], ai[..., 0], bbr, bbi


def _softplus_neg_kernel(lam_ref, o_ref):
    z = -lam_ref[...]
    o_ref[...] = jnp.maximum(z, 0.0) + jnp.log(1.0 + jnp.exp(-jnp.abs(z)))


def _softplus_neg(lam2d):
    return pl.pallas_call(
        _softplus_neg_kernel,
        out_shape=jax.ShapeDtypeStruct(lam2d.shape, F32),
        name="lru_softplus",
    )(lam2d)


def _inproj_kernel(x_ref, sc_ref, sh_ref, wa_ref, wb_ref, wga_ref, wgb_ref, xa_ref, xb_ref, ga_ref, gb_ref):
    tb, bb, d = x_ref.shape
    xm = x_ref[...] * (1.0 + sc_ref[0]) + sh_ref[0]
    xm = xm.reshape(tb * bb, d).astype(BF16)
    for w_ref, o_ref in ((wa_ref, xa_ref), (wb_ref, xb_ref), (wga_ref, ga_ref), (wgb_ref, gb_ref)):
        y = jnp.dot(xm, w_ref[...], preferred_element_type=F32)
        o_ref[...] = y.reshape(tb, bb, y.shape[-1]).astype(o_ref.dtype)


def _inproj(xall, sc, sh, w_a, w_b, w_ga, w_gb, n_lat_t):
    ttot, b, d = xall.shape
    tb = TB_IN
    nl = n_lat_t // tb
    ds = w_b.shape[1]
    mod_spec = pl.BlockSpec((1, b, d), lambda i: (jnp.where(i >= nl, 1, 0), 0, 0))
    tok = lambda c: pl.BlockSpec((tb, b, c), lambda i: (i, 0, 0))
    return pl.pallas_call(
        _inproj_kernel,
        grid=(ttot // tb,),
        in_specs=[tok(d), mod_spec, mod_spec,
                  _const_spec(w_a.shape), _const_spec(w_b.shape), _const_spec(w_ga.shape), _const_spec(w_gb.shape)],
        out_specs=[tok(d), tok(ds), tok(d), tok(d)],
        out_shape=[jax.ShapeDtypeStruct((ttot, b, d), F32), jax.ShapeDtypeStruct((ttot, b, ds), BF16),
                   jax.ShapeDtypeStruct((ttot, b, d), BF16), jax.ShapeDtypeStruct((ttot, b, d), BF16)],
        compiler_params=_cparams(("arbitrary",)),
        name="in_proj",
    )(xall, sc, sh, w_a, w_b, w_ga, w_gb)


def _mixer_direction(reverse, seq_start, any_start, xa_ref, xb_ref, cw_ref, cb_ref, wg_ref, ba_ref, bx_ref,
                     sp_ref, bcat_ref, are_ref, aim_ref, ccat_ref, ext_s, a_s, g_s, hl_s, bu_s, hs_s, ys_s):
    tb, bb, d = xa_ref.shape
    rows = tb * bb
    k_w = cw_ref.shape[0]
    halo = k_w - 1

    @pl.when(any_start)
    def _():
        hl_s[...] = jnp.zeros_like(hl_s)
        hs_s[...] = jnp.zeros_like(hs_s)

    if reverse:
        @pl.when(seq_start)
        def _():
            ext_s[tb:tb + halo] = jnp.zeros((halo, bb, d), F32)

        @pl.when(jnp.logical_not(seq_start))
        def _():
            ext_s[tb:tb + halo] = ext_s[0:halo]

        ext_s[0:tb] = xa_ref[...]
        xc = cb_ref[...] + cw_ref[k_w - 1:k_w] * ext_s[0:tb]
        for k in range(k_w - 1):
            xc = xc + cw_ref[k:k + 1] * ext_s[halo - k:halo - k + tb]
    else:
        @pl.when(seq_start)
        def _():
            ext_s[0:halo] = jnp.zeros((halo, bb, d), F32)

        @pl.when(jnp.logical_not(seq_start))
        def _():
            ext_s[0:halo] = ext_s[tb:tb + halo]

        ext_s[halo:halo + tb] = xa_ref[...]
        xc = cb_ref[...] + cw_ref[k_w - 1:k_w] * ext_s[halo:halo + tb]
        for k in range(k_w - 1):
            xc = xc + cw_ref[k:k + 1] * ext_s[k:k + tb]
    xc = xc.reshape(rows, d)

    nt = wg_ref.shape[0]
    tw = d // nt
    for q in range(nt):
        sl = slice(q * tw, (q + 1) * tw)
        xq = xc[:, sl]
        z = jnp.dot(xq.astype(BF16), wg_ref[q], preferred_element_type=F32)
        r = _sigmoid(z[:, :tw] + ba_ref[:, sl])
        gi = _sigmoid(z[:, tw:] + bx_ref[:, sl])
        log_a = (-LRU_C) * r * sp_ref[:, sl]
        a = jnp.exp(log_a)
        gx = jnp.sqrt(-jnp.tanh(log_a) * (a * a + 1.0)) * (gi * xq)
        a_s[:, :, sl] = a.reshape(tb, bb, tw)
        g_s[:, :, sl] = gx.reshape(tb, bb, tw)

    cl = min(LRU_CHUNK, d)
    for c0 in range(0, d, cl):
        cs = slice(c0, c0 + cl)

        def lru_step(s, h, cs=cs):
            t = tb - 1 - s if reverse else s
            hn = a_s[t, :, cs] * h + g_s[t, :, cs]
            g_s[t, :, cs] = hn
            return hn

        hl_s[:, cs] = lax.fori_loop(0, tb, lru_step, hl_s[:, cs], unroll=True)

    ds = xb_ref.shape[2]
    n_half = bcat_ref.shape[0]
    hw = ds // n_half
    ns = are_ref.shape[2]
    u = xb_ref[...].reshape(rows, ds)
    cw_s5 = min(S5_CHUNK, ns)
    for hf in range(n_half):
        bu = jnp.dot(u[:, hf * hw:(hf + 1) * hw], bcat_ref[hf], preferred_element_type=F32)
        bu_s[...] = bu.reshape(tb, bb, 2 * ns)
        for c0 in range(0, ns, cw_s5):
            rs = slice(c0, c0 + cw_s5)
            is_ = slice(ns + c0, ns + c0 + cw_s5)
            ar = jnp.broadcast_to(are_ref[hf, :, rs], (bb, cw_s5))
            ai = jnp.broadcast_to(aim_ref[hf, :, rs], (bb, cw_s5))

            def s5_step(s, carry, rs=rs, is_=is_, ar=ar, ai=ai):
                hr, hi = carry
                t = tb - 1 - s if reverse else s
                nr = ar * hr - ai * hi + bu_s[t, :, rs]
                ni = ar * hi + ai * hr + bu_s[t, :, is_]
                bu_s[t, :, rs] = nr
                bu_s[t, :, is_] = ni
                return nr, ni

            hr, hi = lax.fori_loop(0, tb, s5_step, (hs_s[hf, :, rs], hs_s[hf, :, is_]), unroll=True)
            hs_s[hf, :, rs] = hr
            hs_s[hf, :, is_] = hi
        hcat = bu_s[...].reshape(rows, 2 * ns).astype(BF16)
        ys_s[:, hf * hw:(hf + 1) * hw] = jnp.dot(hcat, ccat_ref[hf], preferred_element_type=F32)


def _mixer_rev_kernel(xa_ref, xb_ref, cw_ref, cb_ref, wg_ref, ba_ref, bx_ref, sp_ref, bcat_ref, are_ref,
                      aim_ref, ccat_ref, ybl_ref, ybs_ref, ext_s, a_s, g_s, hl_s, bu_s, hs_s, ys_s, *, nc):
    i = pl.program_id(0)
    tb, bb, _ = xa_ref.shape
    _mixer_direction(True, jnp.logical_or(i == 0, i == nc), i == 0, xa_ref, xb_ref, cw_ref, cb_ref, wg_ref,
                     ba_ref, bx_ref, sp_ref, bcat_ref, are_ref, aim_ref, ccat_ref, ext_s, a_s, g_s, hl_s, bu_s,
                     hs_s, ys_s)
    ybl_ref[...] = g_s[...].astype(ybl_ref.dtype)
    ybs_ref[...] = ys_s[...].reshape(tb, bb, ys_s.shape[-1]).astype(ybs_ref.dtype)


def _mixer_fwd_kernel(xa_ref, xb_ref, cw_ref, cb_ref, wg_ref, ba_ref, bx_ref, sp_ref, bcat_ref, are_ref,
                      aim_ref, ccat_ref, ybl_ref, ybs_ref, ga_ref, gb_ref, x_ref, g1_ref, dsk_ref, gluw_ref,
                      glub_ref, pa_ref, pb_ref, wo_ref, lng_ref, lnb_ref, o_ref,
                      ext_s, a_s, g_s, hl_s, bu_s, hs_s, ys_s, *, nc, ctx_out, alpha):
    i = pl.program_id(0)
    tb, bb, d = xa_ref.shape
    rows = tb * bb
    _mixer_direction(False, jnp.logical_or(i == 0, i == nc), i == 0, xa_ref, xb_ref, cw_ref, cb_ref, wg_ref,
                     ba_ref, bx_ref, sp_ref, bcat_ref, are_ref, aim_ref, ccat_ref, ext_s, a_s, g_s, hl_s, bu_s,
                     hs_s, ys_s)

    def merge():
        ds = xb_ref.shape[2]
        ya = (g_s[...] + ybl_ref[...].astype(F32)).reshape(rows, d).astype(BF16)
        u = xb_ref[...].reshape(rows, ds).astype(F32)
        yb = ys_s[...] + ybs_ref[...].reshape(rows, ds).astype(F32) + dsk_ref[...] * u
        z = _gelu_tanh(yb)
        zg = jnp.dot(z.astype(BF16), gluw_ref[...], preferred_element_type=F32) + glub_ref[...]
        yb = (z * _sigmoid(zg)).astype(BF16)
        m = (_sigmoid(ga_ref[...].reshape(rows, d).astype(F32))
             * jnp.dot(ya, pa_ref[...], preferred_element_type=F32)
             + _sigmoid(gb_ref[...].reshape(rows, d).astype(F32))
             * jnp.dot(yb, pb_ref[...], preferred_element_type=F32))
        out = jnp.dot(m.astype(BF16), wo_ref[...], preferred_element_type=F32).reshape(tb, bb, d)
        v = alpha * x_ref[...] + (1.0 + g1_ref[0]) * out
        o_ref[...] = _layer_norm(v, lng_ref[...], lnb_ref[...])

    if ctx_out:
        merge()
    else:
        pl.when(i >= nc)(merge)


def _mixer(xa, xb, ga, gb, xall, g1, lp, n_lat_t, ctx_out, alpha):
    ttot, b, d = xa.shape
    ds = xb.shape[2]
    tb = TB_MIX
    nblk = ttot // tb
    nl = n_lat_t // tb
    nc = nblk - nl
    ns2 = lp["bcat"].shape[-1]
    n_half = lp["bcat"].shape[1]

    scratch = [pltpu.VMEM((tb + lp["conv_w"].shape[1] - 1, b, d), F32),
               pltpu.VMEM((tb, b, d), F32),
               pltpu.VMEM((tb, b, d), F32),
               pltpu.VMEM((b, d), F32),
               pltpu.VMEM((tb, b, ns2), F32),
               pltpu.VMEM((n_half, b, ns2), F32),
               pltpu.VMEM((tb * b, ds), F32)]

    def dir_args(dr):
        return (lp["conv_w"][dr], lp["conv_b"][dr], lp["wg"][dr], lp["ba"][dr], lp["bx"][dr], lp["sp"][dr],
                lp["bcat"][dr], lp["are"][dr], lp["aim"][dr], lp["ccat"][dr])

    def dir_specs(args):
        return [_const_spec(a.shape) for a in args]

    rev = lambda i: (nblk - 1 - i, 0, 0)
    tok_rev = lambda c: pl.BlockSpec((tb, b, c), rev)
    args = dir_args(1)
    ybl, ybs = pl.pallas_call(
        functools.partial(_mixer_rev_kernel, nc=nc),
        grid=(nblk,),
        in_specs=[tok_rev(d), tok_rev(ds)] + dir_specs(args),
        out_specs=[tok_rev(d), tok_rev(ds)],
        out_shape=[jax.ShapeDtypeStruct((ttot, b, d), BF16), jax.ShapeDtypeStruct((ttot, b, ds), BF16)],
        scratch_shapes=scratch,
        compiler_params=_cparams(("arbitrary",)),
        name="mixer_rev",
    )(xa, xb, *args)

    fwd = lambda i: (jnp.where(i < nc, nl + i, i - nc), 0, 0)
    tok_fwd = lambda c: pl.BlockSpec((tb, b, c), fwd)
    if ctx_out:
        out_spec, t_out = tok_fwd(d), ttot
    else:
        out_spec, t_out = pl.BlockSpec((tb, b, d), lambda i: (jnp.maximum(i - nc, 0), 0, 0)), n_lat_t
    g1_spec = pl.BlockSpec((1, b, d), lambda i: (jnp.where(i < nc, 1, 0), 0, 0))
    args = dir_args(0)
    tail = (lp["d_skip"], lp["glu_w"], lp["glu_b"], lp["p_a"], lp["p_b"], lp["w_out"], lp["ln1_g"], lp["ln1_b"])
    return pl.pallas_call(
        functools.partial(_mixer_fwd_kernel, nc=nc, ctx_out=ctx_out, alpha=alpha),
        grid=(nblk,),
        in_specs=([tok_fwd(d), tok_fwd(ds)] + dir_specs(args)
                  + [tok_fwd(d), tok_fwd(ds), tok_fwd(d), tok_fwd(d), tok_fwd(d), g1_spec] + dir_specs(tail)),
        out_specs=out_spec,
        out_shape=jax.ShapeDtypeStruct((t_out, b, d), F32),
        scratch_shapes=scratch,
        compiler_params=_cparams(("arbitrary",)),
        name="mixer_fwd",
    )(xa, xb, *args, ybl, ybs, ga, gb, xall, g1, *tail)


def _ffn_kernel(*refs, img_w, vertical, n_row_blocks, alpha):
    if vertical:
        (x_ref, xt_ref, xbm_ref, sc_ref, sh_ref, g2_ref, wg_ref, wv_ref, wd_ref, cw_ref, cb_ref, lng_ref, lnb_ref,
         o_ref, u_s, gs_s, cv_s) = refs[-17:]
    else:
        (x_ref, sc_ref, sh_ref, g2_ref, wg_ref, wv_ref, wd_ref, cw_ref, cb_ref, lng_ref, lnb_ref,
         o_ref, u_s, gs_s, cv_s) = refs[-15:]
    j, k = pl.program_id(1), pl.program_id(2)
    nt, bb, d = x_ref.shape
    ck = wg_ref.shape[1]
    vo = img_w if vertical else 0
    n_ext = nt + 2 * vo

    @pl.when(k == 0)
    def _():
        mod = lambda r: (r[...] * (1.0 + sc_ref[0]) + sh_ref[0]).astype(BF16)
        u_s[vo:vo + nt] = mod(x_ref)
        if vertical:
            u_s[0:vo] = mod(xt_ref)
            u_s[vo + nt:n_ext] = mod(xbm_ref)
        o_ref[...] = jnp.zeros_like(o_ref)

    gate = jnp.dot(u_s[...].reshape(n_ext * bb, d), wg_ref[...], preferred_element_type=F32)
    gs_s[1:1 + n_ext] = gate.reshape(n_ext, bb, ck)
    zrow = jnp.zeros((1, bb, ck), F32)
    gs_s[0:1] = zrow
    gs_s[1 + n_ext:2 + n_ext] = zrow
    if vertical:
        @pl.when(j == 0)
        def _():
            gs_s[1:1 + vo] = jnp.zeros((vo, bb, ck), F32)

        @pl.when(j == n_row_blocks - 1)
        def _():
            gs_s[1 + vo + nt:1 + n_ext] = jnp.zeros((vo, bb, ck), F32)

    m0 = 1 + vo
    drs = (-1, 0, 1) if vertical else (0,)

    def col_sum(dc):
        acc = None
        for dr in drs:
            s0 = m0 + dr * img_w + dc
            term = cw_ref[(dr + 1) * 3 + dc + 1:(dr + 1) * 3 + dc + 2] * gs_s[s0:s0 + nt]
            acc = term if acc is None else acc + term
        return acc

    left, centre, right = col_sum(-1), col_sum(0), col_sum(1)
    cv_s[...] = centre + left + right
    for r0 in range(0, nt, img_w):
        cv_s[r0] = centre[r0] + right[r0]
        r1 = r0 + img_w - 1
        cv_s[r1] = centre[r1] + left[r1]

    val = jnp.dot(u_s[vo:vo + nt].reshape(nt * bb, d), wv_ref[...], preferred_element_type=F32)
    act = _gelu_tanh(cv_s[...].reshape(nt * bb, ck) + cb_ref[...]) * val
    part = jnp.dot(act.astype(BF16), wd_ref[...], preferred_element_type=F32)
    o_ref[...] += part.reshape(nt, bb, d)

    @pl.when(k == pl.num_programs(2) - 1)
    def _():
        v = alpha * x_ref[...] + (1.0 + g2_ref[0]) * o_ref[...]
        o_ref[...] = _layer_norm(v, lng_ref[...], lnb_ref[...])


def _conv_ffn(xall, sc, sh, g2, lp, n_lat_t, ctx_out, alpha):
    ttot, b, d = xall.shape
    dff = lp["ffn_down"].shape[0]
    ck = min(FFN_CK, dff)
    nk = dff // ck
    bb = min(FFN_BB, b)
    tc = ttot - n_lat_t
    w_up, w_down = lp["ffn_up"], lp["ffn_down"]

    wspecs = [pl.BlockSpec((d, ck), lambda ib, j, k: (0, k)),
              pl.BlockSpec((d, ck), lambda ib, j, k: (0, nk + k)),
              pl.BlockSpec((ck, d), lambda ib, j, k: (k, 0)),
              pl.BlockSpec((9, 1, ck), lambda ib, j, k: (0, 0, k)),
              pl.BlockSpec((1, ck), lambda ib, j, k: (0, k)),
              pl.BlockSpec((1, d), lambda ib, j, k: (0, 0)),
              pl.BlockSpec((1, d), lambda ib, j, k: (0, 0))]
    wargs = (w_up, w_up, w_down, lp["ffn_conv_w"], lp["ffn_conv_b"], lp["ln2_g"], lp["ln2_b"])

    w = GRID_W
    n_img_rows = n_lat_t // w
    r = min(FFN_ROWS, n_img_rows)
    nt = r * w
    nrb = n_img_rows // r
    mod_lat = pl.BlockSpec((1, bb, d), lambda ib, j, k: (0, ib, 0))
    t_out = ttot if ctx_out else n_lat_t
    out = pl.pallas_call(
        functools.partial(_ffn_kernel, img_w=w, vertical=True, n_row_blocks=nrb, alpha=alpha),
        grid=(b // bb, nrb, nk),
        in_specs=[pl.BlockSpec((nt, bb, d), lambda ib, j, k: (j, ib, 0), pipeline_mode=pl.Buffered(1)),
                  pl.BlockSpec((w, bb, d), lambda ib, j, k: (jnp.maximum(j * r - 1, 0), ib, 0),
                               pipeline_mode=pl.Buffered(1)),
                  pl.BlockSpec((w, bb, d), lambda ib, j, k: (jnp.minimum(j * r + r, n_img_rows - 1), ib, 0),
                               pipeline_mode=pl.Buffered(1)),
                  mod_lat, mod_lat, mod_lat] + wspecs,
        out_specs=pl.BlockSpec((nt, bb, d), lambda ib, j, k: (j, ib, 0)),
        out_shape=jax.ShapeDtypeStruct((t_out, b, d), F32),
        scratch_shapes=[pltpu.VMEM((nt + 2 * w, bb, d), BF16),
                        pltpu.VMEM((nt + 2 * w + 2, bb, ck), F32),
                        pltpu.VMEM((nt, bb, ck), F32)],
        compiler_params=_cparams(("arbitrary", "arbitrary", "arbitrary")),
        name="conv_ffn_latent",
    )(xall, xall, xall, sc, sh, g2, *wargs)
    if not ctx_out:
        return out

    cblk = n_lat_t // tc
    mod_ctx = pl.BlockSpec((1, bb, d), lambda ib, j, k: (1, ib, 0))
    return pl.pallas_call(
        functools.partial(_ffn_kernel, img_w=tc, vertical=False, n_row_blocks=1, alpha=alpha),
        grid=(b // bb, 1, nk),
        in_specs=[pl.BlockSpec(memory_space=pl.ANY),
                  pl.BlockSpec((tc, bb, d), lambda ib, j, k: (cblk, ib, 0)),
                  mod_ctx, mod_ctx, mod_ctx] + wspecs,
        out_specs=pl.BlockSpec((tc, bb, d), lambda ib, j, k: (cblk, ib, 0)),
        out_shape=jax.ShapeDtypeStruct((ttot, b, d), F32),
        scratch_shapes=[pltpu.VMEM((tc, bb, d), BF16),
                        pltpu.VMEM((tc + 2, bb, ck), F32),
                        pltpu.VMEM((tc, bb, ck), F32)],
        input_output_aliases={0: 0},
        compiler_params=_cparams(("arbitrary", "arbitrary", "arbitrary")),
        name="conv_ffn_context",
    )(out, xall, sc, sh, g2, *wargs)


def _pack_gate_weights(wa, wx):
    h, hd = wa.shape[-3], wa.shape[-1]
    d = h * hd
    tile = min(MXU_TILE, d)
    hpt = tile // hd
    lead = wa.shape[:-3]
    eye = jnp.eye(hpt, dtype=wa.dtype)

    def dense(w):
        w = w.reshape(lead + (d // tile, hpt, hd, hd))
        return jnp.einsum('...qhij,hg->...qhigj', w, eye).reshape(lead + (d // tile, tile, tile))

    return jnp.concatenate([dense(wa), dense(wx)], axis=-1).astype(BF16)


def _pack_s5(ar, ai, bbr, bbi, c_re, c_im):
    lead = bbr.shape[:-3]
    g, p, s = bbr.shape[-3:]
    gpt = min(MXU_TILE // s, g)
    nh = g // gpt
    eye = jnp.eye(gpt, dtype=F32)

    def b_dense(bb_):
        bb_ = bb_.reshape(lead + (nh, gpt, p, s))
        return jnp.einsum('...hgpc,gk->...hgckp', bb_, eye).reshape(lead + (nh, gpt * s, gpt * p))

    def c_dense(cc):
        cc = cc.reshape(lead + (nh, gpt, s, p))
        return jnp.einsum('...hgcp,gk->...hgpkc', cc, eye).reshape(lead + (nh, gpt * p, gpt * s))

    bcat = jnp.concatenate([b_dense(bbr), b_dense(bbi)], axis=-1).astype(BF16)
    ccat = jnp.concatenate([c_dense(c_re), c_dense(-c_im)], axis=-2).astype(BF16)
    are = ar.reshape(lead + (nh, 1, gpt * p))
    aim = ai.reshape(lead + (nh, 1, gpt * p))
    return bcat, are, aim, ccat


def kernel(x, c, ctx, c_ctx, ada_w, ada_b, w_in, lru_conv_w, lru_conv_b, lru_wa, lru_ba, lru_wx, lru_bx, lru_lam, s5_lam_re, s5_lam_im, s5_log_dt, s5_b_re, s5_b_im, s5_c_re, s5_c_im, s5_d, s5_glu_w, s5_glu_b, p_a, p_b, w_out, ln1_g, ln1_b, ffn_up, ffn_conv_w, ffn_conv_b, ffn_down, ln2_g, ln2_b):
    bsz, seq, d = x.shape
    n_ctx = ctx.shape[1]
    depth = ada_w.shape[0]
    d_lru = lru_lam.shape[-1]
    d_s5 = s5_d.shape[-1]
    dff = ffn_down.shape[1]
    alpha = (2.0 * depth) ** 0.25
    assert seq % n_ctx == 0 and seq % GRID_W == 0 and n_ctx % TB_MIX == 0 and n_ctx % TB_IN == 0

    m_rows = -(-(bsz + 1) // 8) * 8
    cond = jnp.zeros((m_rows, d), F32).at[:bsz].set(c).at[bsz].set(c_ctx)
    modv = _modulation(cond, ada_w, ada_b).reshape(depth, m_rows, 6, d)

    ar, ai, bbr, bbi = _s5_discretise(s5_lam_re, s5_lam_im, s5_log_dt, s5_b_re, s5_b_im)
    bcat, are, aim, ccat = _pack_s5(ar, ai, bbr, bbi, s5_c_re, s5_c_im)
    sp = _softplus_neg(lru_lam.reshape(depth * 2, d_lru)).reshape(depth, 2, 1, d_lru)
    wg = _pack_gate_weights(lru_wa, lru_wx)
    w_in16 = w_in.astype(BF16)
    row = lambda a: a[..., None, :]

    xall = jnp.concatenate([jnp.transpose(x, (1, 0, 2)), jnp.transpose(ctx, (1, 0, 2))], axis=0)
    for l in range(depth):
        ctx_out = l < depth - 1
        lat = modv[l, :bsz]
        cx = jnp.broadcast_to(modv[l, bsz][None], (bsz, 6, d))
        mods = jnp.stack([lat, cx], axis=0)
        sh1, sc1, g1, sh2, sc2, g2 = (mods[:, :, n] for n in range(6))
        lp = dict(conv_w=lru_conv_w[l][:, :, None, :], conv_b=row(lru_conv_b[l]), wg=wg[l], ba=row(lru_ba[l]),
                  bx=row(lru_bx[l]), sp=sp[l], bcat=bcat[l], are=are[l], aim=aim[l], ccat=ccat[l],
                  d_skip=row(s5_d[l]), glu_w=s5_glu_w[l].astype(BF16), glu_b=row(s5_glu_b[l]),
                  p_a=p_a[l].astype(BF16), p_b=p_b[l].astype(BF16), w_out=w_out[l].astype(BF16),
                  ln1_g=row(ln1_g[l]), ln1_b=row(ln1_b[l]),
                  ffn_up=ffn_up[l].astype(BF16), ffn_down=ffn_down[l].astype(BF16),
                  ffn_conv_w=ffn_conv_w[l].reshape(9, 1, dff), ffn_conv_b=row(ffn_conv_b[l]),
                  ln2_g=row(ln2_g[l]), ln2_b=row(ln2_b[l]))
        wl = w_in16[l]
        xa, xb, ga, gb = _inproj(xall, sc1, sh1, wl[:, :d_lru], wl[:, d_lru:d_lru + d_s5],
                                 wl[:, d_lru + d_s5:d_lru + d_s5 + d], wl[:, d_lru + d_s5 + d:], seq)
        xmid =_mixer(xa, xb, ga, gb, xall, g1, lp, seq, ctx_out, alpha)
        xall = _conv_ffn(xmid, sc2, sh2, g2, lp, seq, ctx_out, alpha)
    return jnp.transpose(xall[:seq], (1, 0, 2))
```

```python
import functools
import math

import jax
import jax.numpy as jnp
from jax import lax
from jax.experimental import pallas as pl
from jax.experimental.pallas import tpu as pltpu

F32 = jnp.float32
BF16 = jnp.bfloat16

GRID_W = 64
LRU_C = 8.0
LN_EPS = 1e-5
S5_GROUP = 16
MXU_TILE = 256
VMEM_LIMIT = 60 * 1024 * 1024

TB_IN = 16
TB_MIX = 16
FFN_ROWS = 4
FFN_BB = 8
FFN_CK = 512
LRU_CHUNK = 512
S5_CHUNK = 256


def _cparams(sem):
    return pltpu.CompilerParams(dimension_semantics=sem, vmem_limit_bytes=VMEM_LIMIT)


def _const_spec(shape):
    nd = len(shape)
    return pl.BlockSpec(shape, lambda *_: (0,) * nd, pipeline_mode=pl.Buffered(1))


def _gelu_tanh(x):
    return 0.5 * x * (1.0 + jnp.tanh(math.sqrt(2.0 / math.pi) * (x + 0.044715 * (x * x * x))))


def _sigmoid(x):
    return 1.0 / (1.0 + jnp.exp(-x))


def _layer_norm(v, g, b):
    mu = jnp.mean(v, axis=-1, keepdims=True)
    xc = v - mu
    var = jnp.mean(xc * xc, axis=-1, keepdims=True)
    return xc * lax.rsqrt(var + LN_EPS) * g + b


def _mod_kernel(c_ref, w_ref, b_ref, o_ref):
    c = c_ref[...]
    s = (c * _sigmoid(c)).astype(BF16)
    o_ref[0] = jnp.dot(s, w_ref[0].astype(BF16), preferred_element_type=F32) + b_ref[0]


def _modulation(cond, ada_w, ada_b):
    depth, d, n = ada_w.shape
    m = cond.shape[0]
    tn = min(n, 1536)
    return pl.pallas_call(
        _mod_kernel,
        grid=(depth, n // tn),
        in_specs=[pl.BlockSpec((m, d), lambda l, j: (0, 0)),
                  pl.BlockSpec((1, d, tn), lambda l, j: (l, 0, j)),
                  pl.BlockSpec((1, 1, tn), lambda l, j: (l, 0, j))],
        out_specs=pl.BlockSpec((1, m, tn), lambda l, j: (l, 0, j)),
        out_shape=jax.ShapeDtypeStruct((depth, m, n), F32),
        compiler_params=_cparams(("arbitrary", "arbitrary")),
        name="adaln_mod",
    )(cond, ada_w, ada_b.reshape(depth, 1, n))


def _s5_disc_kernel(lr_ref, li_ref, ldt_ref, br_ref, bi_ref, ar_ref, ai_ref, bbr_ref, bbi_ref):
    lr, li = lr_ref[...], li_ref[...]
    dt = jnp.exp(ldt_ref[...])
    mag = jnp.exp(lr * dt)
    ar = mag * jnp.cos(li * dt)
    ai = mag * jnp.sin(li * dt)
    den = lr * lr + li * li
    fr = ((ar - 1.0) * lr + ai * li) / den
    fi = (ai * lr - (ar - 1.0) * li) / den
    br, bi = br_ref[...], bi_ref[...]
    ar_ref[...] = ar
    ai_ref[...] = ai
    bbr_ref[...] = fr * br - fi * bi
    bbi_ref[...] = fr * bi + fi * br


def _s5_discretise(lam_re, lam_im, log_dt, b_re, b_im):
    lead = b_re.shape[:-2]
    p, s = b_re.shape[-2:]
    n = math.prod(lead)
    full = lead + (p, s)
    flat = lambda a: jnp.broadcast_to(a, full).reshape(n, p * s)
    args = (flat(lam_re[..., None]), flat(lam_im[..., None]), flat(log_dt[..., None, None]), flat(b_re), flat(b_im))
    rb = 8 if n % 8 == 0 else n
    spec = pl.BlockSpec((rb, p * s), lambda i: (i, 0))
    outs = pl.pallas_call(
        _s5_disc_kernel,
        grid=(n // rb,),
        in_specs=[spec] * 5,
        out_specs=[spec] * 4,
        out_shape=[jax.ShapeDtypeStruct((n, p * s), F32)] * 4,
        compiler_params=_cparams(("arbitrary",)),
        name="s5_discretise",
    )(*args)
    ar, ai, bbr, bbi = (o.reshape(full) for o in outs)
    return ar[..., 0], ai[..., 0], bbr, bbi


def _softplus_neg_kernel(lam_ref, o_ref):
    z = -lam_ref[...]
    o_ref[...] = jnp.maximum(z, 0.0) + jnp.log(1.0 + jnp.exp(-jnp.abs(z)))


def _softplus_neg(lam2d):
    return pl.pallas_call(
        _softplus_neg_kernel,
        out_shape=jax.ShapeDtypeStruct(lam2d.shape, F32),
        name="lru_softplus",
    )(lam2d)


def _inproj_kernel(x_ref, sc_ref, sh_ref, wa_ref, wb_ref, wga_ref, wgb_ref, xa_ref, xb_ref, ga_ref, gb_ref):
    tb, bb, d = x_ref.shape
    xm = x_ref[...] * (1.0 + sc_ref[0]) + sh_ref[0]
    xm = xm.reshape(tb * bb, d).astype(BF16)
    for w_ref, o_ref in ((wa_ref, xa_ref), (wb_ref, xb_ref), (wga_ref, ga_ref), (wgb_ref, gb_ref)):
        y = jnp.dot(xm, w_ref[...], preferred_element_type=F32)
        o_ref[...] = y.reshape(tb, bb, y.shape[-1]).astype(o_ref.dtype)


def _inproj(xall, sc, sh, w_a, w_b, w_ga, w_gb, n_lat_t):
    ttot, b, d = xall.shape
    tb = TB_IN
    nl = n_lat_t // tb
    ds = w_b.shape[1]
    mod_spec = pl.BlockSpec((1, b, d), lambda i: (jnp.where(i >= nl, 1, 0), 0, 0))
    tok = lambda c: pl.BlockSpec((tb, b, c), lambda i: (i, 0, 0))
    return pl.pallas_call(
        _inproj_kernel,
        grid=(ttot // tb,),
        in_specs=[tok(d), mod_spec, mod_spec,
                  _const_spec(w_a.shape), _const_spec(w_b.shape), _const_spec(w_ga.shape), _const_spec(w_gb.shape)],
        out_specs=[tok(d), tok(ds), tok(d), tok(d)],
        out_shape=[jax.ShapeDtypeStruct((ttot, b, d), F32), jax.ShapeDtypeStruct((ttot, b, ds), BF16),
                   jax.ShapeDtypeStruct((ttot, b, d), BF16), jax.ShapeDtypeStruct((ttot, b, d), BF16)],
        compiler_params=_cparams(("arbitrary",)),
        name="in_proj",
    )(xall, sc, sh, w_a, w_b, w_ga, w_gb)


def _mixer_direction(reverse, seq_start, any_start, xa_ref, xb_ref, cw_ref, cb_ref, wg_ref, ba_ref, bx_ref,
                     sp_ref, bcat_ref, are_ref, aim_ref, ccat_ref, ext_s, a_s, g_s, hl_s, bu_s, hs_s, ys_s):
    tb, bb, d = xa_ref.shape
    rows = tb * bb
    k_w = cw_ref.shape[0]
    halo = k_w - 1

    @pl.when(any_start)
    def _():
        hl_s[...] = jnp.zeros_like(hl_s)
        hs_s[...] = jnp.zeros_like(hs_s)

    if reverse:
        @pl.when(seq_start)
        def _():
            ext_s[tb:tb + halo] = jnp.zeros((halo, bb, d), F32)

        @pl.when(jnp.logical_not(seq_start))
        def _():
            ext_s[tb:tb + halo] = ext_s[0:halo]

        ext_s[0:tb] = xa_ref[...]
        xc = cb_ref[...] + cw_ref[k_w - 1:k_w] * ext_s[0:tb]
        for k in range(k_w - 1):
            xc = xc + cw_ref[k:k + 1] * ext_s[halo - k:halo - k + tb]
    else:
        @pl.when(seq_start)
        def _():
            ext_s[0:halo] = jnp.zeros((halo, bb, d), F32)

        @pl.when(jnp.logical_not(seq_start))
        def _():
            ext_s[0:halo] = ext_s[tb:tb + halo]

        ext_s[halo:halo + tb] = xa_ref[...]
        xc = cb_ref[...] + cw_ref[k_w - 1:k_w] * ext_s[halo:halo + tb]
        for k in range(k_w - 1):
            xc = xc + cw_ref[k:k + 1] * ext_s[k:k + tb]
    xc = xc.reshape(rows, d)

    ds = xb_ref.shape[2]
    n_half = bcat_ref.shape[0]
    hw = ds // n_half
    ns = are_ref.shape[2]
    cw_s5 = min(S5_CHUNK, ns)
    u = xb_ref[...].reshape(rows, ds)
    for hf in range(n_half):
        bu = jnp.dot(u[:, hf * hw:(hf + 1) * hw], bcat_ref[hf], preferred_element_type=F32)
        bu_s[hf] = bu.reshape(tb, bb, 2 * ns)

    nt = wg_ref.shape[0]
    tw = d // nt
    for q in range(nt):
        sl = slice(q * tw, (q + 1) * tw)
        xq = xc[:, sl]
        z = jnp.dot(xq.astype(BF16), wg_ref[q], preferred_element_type=F32)
        r = _sigmoid(z[:, :tw] + ba_ref[:, sl])
        gi = _sigmoid(z[:, tw:] + bx_ref[:, sl])
        log_a = (-LRU_C) * r * sp_ref[:, sl]
        a = jnp.exp(log_a)
        gx = jnp.sqrt(-jnp.tanh(log_a) * (a * a + 1.0)) * (gi * xq)
        a_s[:, :, sl] = a.reshape(tb, bb, tw)
        g_s[:, :, sl] = gx.reshape(tb, bb, tw)

    def s5_scan(hf):
        for c0 in range(0, ns, cw_s5):
            rs = slice(c0, c0 + cw_s5)
            is_ = slice(ns + c0, ns + c0 + cw_s5)
            ar = jnp.broadcast_to(are_ref[hf, :, rs], (bb, cw_s5))
            ai = jnp.broadcast_to(aim_ref[hf, :, rs], (bb, cw_s5))

            def s5_step(s, carry, rs=rs, is_=is_, ar=ar, ai=ai):
                hr, hi = carry
                t = tb - 1 - s if reverse else s
                nr = ar * hr - ai * hi + bu_s[hf, t, :, rs]
                ni = ar * hi + ai * hr + bu_s[hf, t, :, is_]
                bu_s[hf, t, :, rs] = nr
                bu_s[hf, t, :, is_] = ni
                return nr, ni

            hr, hi = lax.fori_loop(0, tb, s5_step, (hs_s[hf, :, rs], hs_s[hf, :, is_]), unroll=True)
            hs_s[hf, :, rs] = hr
            hs_s[hf, :, is_] = hi

    def s5_readout(hf):
        hcat = bu_s[hf].reshape(rows, 2 * ns).astype(BF16)
        ys_s[:, hf * hw:(hf + 1) * hw] = jnp.dot(hcat, ccat_ref[hf], preferred_element_type=F32)

    def lru_scan():
        cl = min(LRU_CHUNK, d)
        for c0 in range(0, d, cl):
            cs = slice(c0, c0 + cl)

            def lru_step(s, h, cs=cs):
                t = tb - 1 - s if reverse else s
                hn = a_s[t, :, cs] * h + g_s[t, :, cs]
                g_s[t, :, cs] = hn
                return hn

            hl_s[:, cs] = lax.fori_loop(0, tb, lru_step, hl_s[:, cs], unroll=True)

    for hf in range(n_half):
        s5_scan(hf)
        s5_readout(hf)
    lru_scan()


def _mixer_rev_kernel(xa_ref, xb_ref, cw_ref, cb_ref, wg_ref, ba_ref, bx_ref, sp_ref, bcat_ref, are_ref,
                      aim_ref, ccat_ref, ybl_ref, ybs_ref, ext_s, a_s, g_s, hl_s, bu_s, hs_s, ys_s, *, nc):
    i = pl.program_id(0)
    tb, bb, _ = xa_ref.shape
    _mixer_direction(True, jnp.logical_or(i == 0, i == nc), i == 0, xa_ref, xb_ref, cw_ref, cb_ref, wg_ref,
                     ba_ref, bx_ref, sp_ref, bcat_ref, are_ref, aim_ref, ccat_ref, ext_s, a_s, g_s, hl_s, bu_s,
                     hs_s, ys_s)
    ybl_ref[...] = g_s[...].astype(ybl_ref.dtype)
    ybs_ref[...] = ys_s[...].reshape(tb, bb, ys_s.shape[-1]).astype(ybs_ref.dtype)


def _mixer_fwd_kernel(xa_ref, xb_ref, cw_ref, cb_ref, wg_ref, ba_ref, bx_ref, sp_ref, bcat_ref, are_ref,
                      aim_ref, ccat_ref, ybl_ref, ybs_ref, ga_ref, gb_ref, x_ref, g1_ref, dsk_ref, gluw_ref,
                      glub_ref, pa_ref, pb_ref, wo_ref, lng_ref, lnb_ref, o_ref,
                      ext_s, a_s, g_s, hl_s, bu_s, hs_s, ys_s, *, nc, ctx_out, alpha):
    i = pl.program_id(0)
    tb, bb, d = xa_ref.shape
    rows = tb * bb
    _mixer_direction(False, jnp.logical_or(i == 0, i == nc), i == 0, xa_ref, xb_ref, cw_ref, cb_ref, wg_ref,
                     ba_ref, bx_ref, sp_ref, bcat_ref, are_ref, aim_ref, ccat_ref, ext_s, a_s, g_s, hl_s, bu_s,
                     hs_s, ys_s)

    def merge():
        ds = xb_ref.shape[2]
        ya = (g_s[...] + ybl_ref[...].astype(F32)).reshape(rows, d).astype(BF16)
        u = xb_ref[...].reshape(rows, ds).astype(F32)
        yb = ys_s[...] + ybs_ref[...].reshape(rows, ds).astype(F32) + dsk_ref[...] * u
        z = _gelu_tanh(yb)
        zg = jnp.dot(z.astype(BF16), gluw_ref[...], preferred_element_type=F32) + glub_ref[...]
        yb = (z * _sigmoid(zg)).astype(BF16)
        m = (_sigmoid(ga_ref[...].reshape(rows, d).astype(F32))
             * jnp.dot(ya, pa_ref[...], preferred_element_type=F32)
             + _sigmoid(gb_ref[...].reshape(rows, d).astype(F32))
             * jnp.dot(yb, pb_ref[...], preferred_element_type=F32))
        out = jnp.dot(m.astype(BF16), wo_ref[...], preferred_element_type=F32).reshape(tb, bb, d)
        v = alpha * x_ref[...] + (1.0 + g1_ref[0]) * out
        o_ref[...] = _layer_norm(v, lng_ref[...], lnb_ref[...])

    if ctx_out:
        merge()
    else:
        pl.when(i >= nc)(merge)


def _mixer(xa, xb, ga, gb, xall, g1, lp, n_lat_t, ctx_out, alpha):
    ttot, b, d = xa.shape
    ds = xb.shape[2]
    tb = TB_MIX
    nblk = ttot // tb
    nl = n_lat_t // tb
    nc = nblk - nl
    ns2 = lp["bcat"].shape[-1]
    n_half = lp["bcat"].shape[1]

    scratch = [pltpu.VMEM((tb + lp["conv_w"].shape[1] - 1, b, d), F32),
               pltpu.VMEM((tb, b, d), F32),
               pltpu.VMEM((tb, b, d), F32),
               pltpu.VMEM((b, d), F32),
               pltpu.VMEM((n_half, tb, b, ns2), F32),
               pltpu.VMEM((n_half, b, ns2), F32),
               pltpu.VMEM((tb * b, ds), F32)]

    def dir_args(dr):
        return (lp["conv_w"][dr], lp["conv_b"][dr], lp["wg"][dr], lp["ba"][dr], lp["bx"][dr], lp["sp"][dr],
                lp["bcat"][dr], lp["are"][dr], lp["aim"][dr], lp["ccat"][dr])

    def dir_specs(args):
        return [_const_spec(a.shape) for a in args]

    rev = lambda i: (nblk - 1 - i, 0, 0)
    tok_rev = lambda c: pl.BlockSpec((tb, b, c), rev)
    args = dir_args(1)
    ybl, ybs = pl.pallas_call(
        functools.partial(_mixer_rev_kernel, nc=nc),
        grid=(nblk,),
        in_specs=[tok_rev(d), tok_rev(ds)] + dir_specs(args),
        out_specs=[tok_rev(d), tok_rev(ds)],
        out_shape=[jax.ShapeDtypeStruct((ttot, b, d), BF16), jax.ShapeDtypeStruct((ttot, b, ds), BF16)],
        scratch_shapes=scratch,
        compiler_params=_cparams(("arbitrary",)),
        name="mixer_rev",
    )(xa, xb, *args)

    fwd = lambda i: (jnp.where(i < nc, nl + i, i - nc), 0, 0)
    tok_fwd = lambda c: pl.BlockSpec((tb, b, c), fwd)
    if ctx_out:
        out_spec, t_out = tok_fwd(d), ttot
    else:
        out_spec, t_out = pl.BlockSpec((tb, b, d), lambda i: (jnp.maximum(i - nc, 0), 0, 0)), n_lat_t
    g1_spec = pl.BlockSpec((1, b, d), lambda i: (jnp.where(i < nc, 1, 0), 0, 0))
    args = dir_args(0)
    tail = (lp["d_skip"], lp["glu_w"], lp["glu_b"], lp["p_a"], lp["p_b"], lp["w_out"], lp["ln1_g"], lp["ln1_b"])
    return pl.pallas_call(
        functools.partial(_mixer_fwd_kernel, nc=nc, ctx_out=ctx_out, alpha=alpha),
        grid=(nblk,),
        in_specs=([tok_fwd(d), tok_fwd(ds)] + dir_specs(args)
                  + [tok_fwd(d), tok_fwd(ds), tok_fwd(d), tok_fwd(d), tok_fwd(d), g1_spec] + dir_specs(tail)),
        out_specs=out_spec,
        out_shape=jax.ShapeDtypeStruct((t_out, b, d), F32),
        scratch_shapes=scratch,
        compiler_params=_cparams(("arbitrary",)),
        name="mixer_fwd",
    )(xa, xb, *args, ybl, ybs, ga, gb, xall, g1, *tail)


def _ffn_kernel(*refs, piece, vertical, n_row_blocks, alpha):
    if vertical:
        (x_ref, xt_ref, xbm_ref, sc_ref, sh_ref, g2_ref, wg_ref, wv_ref, wd_ref, cw_ref, cb_ref, lng_ref, lnb_ref,
         o_ref, u_s, gs_s) = refs[-16:]
    else:
        (x_ref, sc_ref, sh_ref, g2_ref, wg_ref, wv_ref, wd_ref, cw_ref, cb_ref, lng_ref, lnb_ref,
         o_ref, u_s, gs_s) = refs[-14:]
    j, k = pl.program_id(1), pl.program_id(2)
    nt, bb, d = x_ref.shape
    ck = wg_ref.shape[1]
    n_pre = 1 if vertical else 0
    n_main = nt // piece
    n_ext = n_main + 2 * n_pre
    vo = n_pre * piece
    stride = piece + 1 if vertical else piece

    @pl.when(k == 0)
    def _():
        mod = lambda r: (r[...] * (1.0 + sc_ref[0]) + sh_ref[0]).astype(BF16)
        u_s[vo:vo + nt] = mod(x_ref)
        if vertical:
            u_s[0:vo] = mod(xt_ref)
            u_s[vo + nt:vo + nt + vo] = mod(xbm_ref)
        o_ref[...] = jnp.zeros_like(o_ref)

    for p in (range(n_ext + 1) if vertical else (0,)):
        gs_s[p * stride] = jnp.zeros((bb, ck), F32)
    if not vertical:
        gs_s[1 + nt] = jnp.zeros((bb, ck), F32)

    def gate_piece(e):
        g = jnp.dot(u_s[e * piece:(e + 1) * piece].reshape(piece * bb, d), wg_ref[...],
                    preferred_element_type=F32)
        if vertical and e == 0:
            g = jnp.where(j == 0, 0.0, g)
        if vertical and e == n_ext - 1:
            g = jnp.where(j == n_row_blocks - 1, 0.0, g)
        gs_s[1 + e * stride:1 + e * stride + piece] = g.reshape(piece, bb, ck)

    drs = (-1, 0, 1) if vertical else (0,)
    gates_done = 0
    for r in range(n_main):
        need = min(r + n_pre + 1, n_ext - 1)
        while gates_done <= need:
            gate_piece(gates_done)
            gates_done += 1
        val = jnp.dot(u_s[vo + r * piece:vo + (r + 1) * piece].reshape(piece * bb, d), wv_ref[...],
                      preferred_element_type=F32)
        conv = None
        for dr in drs:
            for dc in (-1, 0, 1):
                s0 = 1 + (r + n_pre + dr) * stride + dc
                tap = (dr + 1) * 3 + dc + 1
                term = cw_ref[tap:tap + 1] * gs_s[s0:s0 + piece]
                conv = term if conv is None else conv + term
        act = _gelu_tanh(conv.reshape(piece * bb, ck) + cb_ref[...]) * val
        part = jnp.dot(act.astype(BF16), wd_ref[...], preferred_element_type=F32)
        o_ref[r * piece:(r + 1) * piece] += part.reshape(piece, bb, d)

    @pl.when(k == pl.num_programs(2) - 1)
    def _():
        v = alpha * x_ref[...] + (1.0 + g2_ref[0]) * o_ref[...]
        o_ref[...] = _layer_norm(v, lng_ref[...], lnb_ref[...])


def _conv_ffn(xall, sc, sh, g2, lp, n_lat_t, ctx_out, alpha):
    ttot, b, d = xall.shape
    dff = lp["ffn_down"].shape[0]
    ck = min(FFN_CK, dff)
    nk = dff // ck
    bb = min(FFN_BB, b)
    tc = ttot - n_lat_t
    w_up, w_down = lp["ffn_up"], lp["ffn_down"]

    wspecs = [pl.BlockSpec((d, ck), lambda ib, j, k: (0, k)),
              pl.BlockSpec((d, ck), lambda ib, j, k: (0, nk + k)),
              pl.BlockSpec((ck, d), lambda ib, j, k: (k, 0)),
              pl.BlockSpec((9, 1, ck), lambda ib, j, k: (0, 0, k)),
              pl.BlockSpec((1, ck), lambda ib, j, k: (0, k)),
              pl.BlockSpec((1, d), lambda ib, j, k: (0, 0)),
              pl.BlockSpec((1, d), lambda ib, j, k: (0, 0))]
    wargs = (w_up, w_up, w_down, lp["ffn_conv_w"], lp["ffn_conv_b"], lp["ln2_g"], lp["ln2_b"])

    w = GRID_W
    n_img_rows = n_lat_t // w
    r = min(FFN_ROWS, n_img_rows)
    nt = r * w
    nrb = n_img_rows // r
    mod_lat = pl.BlockSpec((1, bb, d), lambda ib, j, k: (0, ib, 0))
    t_out = ttot if ctx_out else n_lat_t
    out = pl.pallas_call(
        functools.partial(_ffn_kernel, piece=w, vertical=True, n_row_blocks=nrb, alpha=alpha),
        grid=(b // bb, nrb, nk),
        in_specs=[pl.BlockSpec((nt, bb, d), lambda ib, j, k: (j, ib, 0), pipeline_mode=pl.Buffered(1)),
                  pl.BlockSpec((w, bb, d), lambda ib, j, k: (jnp.maximum(j * r - 1, 0), ib, 0),
                               pipeline_mode=pl.Buffered(1)),
                  pl.BlockSpec((w, bb, d), lambda ib, j, k: (jnp.minimum(j * r + r, n_img_rows - 1), ib, 0),
                               pipeline_mode=pl.Buffered(1)),
                  mod_lat, mod_lat, mod_lat] + wspecs,
        out_specs=pl.BlockSpec((nt, bb, d), lambda ib, j, k: (j, ib, 0)),
        out_shape=jax.ShapeDtypeStruct((t_out, b, d), F32),
        scratch_shapes=[pltpu.VMEM((nt + 2 * w, bb, d), BF16),
                        pltpu.VMEM(((r + 2) * (w + 1) + 1, bb, ck), F32)],
        compiler_params=_cparams(("arbitrary", "arbitrary", "arbitrary")),
        name="conv_ffn_latent",
    )(xall, xall, xall, sc, sh, g2, *wargs)
    if not ctx_out:
        return out

    cblk = n_lat_t // tc
    mod_ctx = pl.BlockSpec((1, bb, d), lambda ib, j, k: (1, ib, 0))
    return pl.pallas_call(
        functools.partial(_ffn_kernel, piece=min(w, tc), vertical=False, n_row_blocks=1, alpha=alpha),
        grid=(b // bb, 1, nk),
        in_specs=[pl.BlockSpec(memory_space=pl.ANY),
                  pl.BlockSpec((tc, bb, d), lambda ib, j, k: (cblk, ib, 0), pipeline_mode=pl.Buffered(1)),
                  mod_ctx, mod_ctx, mod_ctx] + wspecs,
        out_specs=pl.BlockSpec((tc, bb, d), lambda ib, j, k: (cblk, ib, 0)),
        out_shape=jax.ShapeDtypeStruct((ttot, b, d), F32),
        scratch_shapes=[pltpu.VMEM((tc, bb, d), BF16),
                        pltpu.VMEM((tc + 2, bb, ck), F32)],
        input_output_aliases={0: 0},
        compiler_params=_cparams(("arbitrary", "arbitrary", "arbitrary")),
        name="conv_ffn_context",
    )(out, xall, sc, sh, g2, *wargs)


def _pack_gate_weights(wa, wx):
    h, hd = wa.shape[-3], wa.shape[-1]
    d = h * hd
    tile = min(MXU_TILE, d)
    hpt = tile // hd
    lead = wa.shape[:-3]
    eye = jnp.eye(hpt, dtype=wa.dtype)

    def dense(w):
        w = w.reshape(lead + (d // tile, hpt, hd, hd))
        return jnp.einsum('...qhij,hg->...qhigj', w, eye).reshape(lead + (d // tile, tile, tile))

    return jnp.concatenate([dense(wa), dense(wx)], axis=-1).astype(BF16)


def _pack_s5(ar, ai, bbr, bbi, c_re, c_im):
    lead = bbr.shape[:-3]
    g, p, s = bbr.shape[-3:]
    gpt = min(MXU_TILE // s, g)
    nh = g // gpt
    eye = jnp.eye(gpt, dtype=F32)

    def b_dense(bb_):
        bb_ = bb_.reshape(lead + (nh, gpt, p, s))
        return jnp.einsum('...hgpc,gk->...hgckp', bb_, eye).reshape(lead + (nh, gpt * s, gpt * p))

    def c_dense(cc):
        cc = cc.reshape(lead + (nh, gpt, s, p))
        return jnp.einsum('...hgcp,gk->...hgpkc', cc, eye).reshape(lead + (nh, gpt * p, gpt * s))

    bcat = jnp.concatenate([b_dense(bbr), b_dense(bbi)], axis=-1).astype(BF16)
    ccat = jnp.concatenate([c_dense(c_re), c_dense(-c_im)], axis=-2).astype(BF16)
    are = ar.reshape(lead + (nh, 1, gpt * p))
    aim = ai.reshape(lead + (nh, 1, gpt * p))
    return bcat, are, aim, ccat


def kernel(x, c, ctx, c_ctx, ada_w, ada_b, w_in, lru_conv_w, lru_conv_b, lru_wa, lru_ba, lru_wx, lru_bx, lru_lam, s5_lam_re, s5_lam_im, s5_log_dt, s5_b_re, s5_b_im, s5_c_re, s5_c_im, s5_d, s5_glu_w, s5_glu_b, p_a, p_b, w_out, ln1_g, ln1_b, ffn_up, ffn_conv_w, ffn_conv_b, ffn_down, ln2_g, ln2_b):
    bsz, seq, d = x.shape
    n_ctx = ctx.shape[1]
    depth = ada_w.shape[0]
    d_lru = lru_lam.shape[-1]
    d_s5 = s5_d.shape[-1]
    dff = ffn_down.shape[1]
    alpha = (2.0 * depth) ** 0.25
    assert seq % n_ctx == 0 and seq % GRID_W == 0 and n_ctx % TB_MIX == 0 and n_ctx % TB_IN == 0

    m_rows = -(-(bsz + 1) // 8) * 8
    cond = jnp.zeros((m_rows, d), F32).at[:bsz].set(c).at[bsz].set(c_ctx)
    modv = _modulation(cond, ada_w, ada_b).reshape(depth, m_rows, 6, d)

    ar, ai, bbr, bbi = _s5_discretise(s5_lam_re, s5_lam_im, s5_log_dt, s5_b_re, s5_b_im)
    bcat, are, aim, ccat = _pack_s5(ar, ai, bbr, bbi, s5_c_re, s5_c_im)
    sp = _softplus_neg(lru_lam.reshape(depth * 2, d_lru)).reshape(depth, 2, 1, d_lru)
    wg = _pack_gate_weights(lru_wa, lru_wx)
    w_in16 = w_in.astype(BF16)
    row = lambda a: a[..., None, :]

    xall = jnp.concatenate([jnp.transpose(x, (1, 0, 2)), jnp.transpose(ctx, (1, 0, 2))], axis=0)
    for l in range(depth):
        ctx_out = l < depth - 1
        lat = modv[l, :bsz]
        cx = jnp.broadcast_to(modv[l, bsz][None], (bsz, 6, d))
        mods = jnp.stack([lat, cx], axis=0)
        sh1, sc1, g1, sh2, sc2, g2 = (mods[:, :, n] for n in range(6))
        lp = dict(conv_w=lru_conv_w[l][:, :, None, :], conv_b=row(lru_conv_b[l]), wg=wg[l], ba=row(lru_ba[l]),
                  bx=row(lru_bx[l]), sp=sp[l], bcat=bcat[l], are=are[l], aim=aim[l], ccat=ccat[l],
                  d_skip=row(s5_d[l]), glu_w=s5_glu_w[l].astype(BF16), glu_b=row(s5_glu_b[l]),
                  p_a=p_a[l].astype(BF16), p_b=p_b[l].astype(BF16), w_out=w_out[l].astype(BF16),
                  ln1_g=row(ln1_g[l]), ln1_b=row(ln1_b[l]),
                  ffn_up=ffn_up[l].astype(BF16), ffn_down=ffn_down[l].astype(BF16),
                  ffn_conv_w=ffn_conv_w[l].reshape(9, 1, dff), ffn_conv_b=row(ffn_conv_b[l]),
                  ln2_g=row(ln2_g[l]), ln2_b=row(ln2_b[l]))
        wl = w_in16[l]
        xa, xb, ga, gb = _inproj(xall, sc1, sh1, wl[:, :d_lru], wl[:, d_lru:d_lru + d_s5],
                                 wl[:, d_lru + d_s5:d_lru + d_s5 + d], wl[:, d_lru + d_s5 + d:], seq)
        xmid =_mixer(xa, xb, ga, gb, xall, g1, lp, seq, ctx_out, alpha)
        xall = _conv_ffn(xmid, sc2, sh2, g2, lp, seq, ctx_out, alpha)
    return jnp.transpose(xall[:seq], (1, 0, 2))
```

```python
import functools
import math

import jax
import jax.numpy as jnp
from jax import lax
from jax.experimental import pallas as pl
from jax.experimental.pallas import tpu as pltpu

F32 = jnp.float32
BF16 = jnp.bfloat16

GRID_W = 64
LRU_C = 8.0
LN_EPS = 1e-5
S5_GROUP = 16
MXU_TILE = 256
VMEM_LIMIT = 60 * 1024 * 1024

TB_IN = 16
TB_MIX = 16
FFN_ROWS = 4
FFN_BB = 8
FFN_CK = 512
LRU_CHUNK = 512
S5_CHUNK = 256


def _cparams(sem):
    return pltpu.CompilerParams(dimension_semantics=sem, vmem_limit_bytes=VMEM_LIMIT)


def _const_spec(shape):
    nd = len(shape)
    return pl.BlockSpec(shape, lambda *_: (0,) * nd, pipeline_mode=pl.Buffered(1))


def _gelu_tanh(x):
    return 0.5 * x * (1.0 + jnp.tanh(math.sqrt(2.0 / math.pi) * (x + 0.044715 * (x * x * x))))


def _sigmoid(x):
    return 1.0 / (1.0 + jnp.exp(-x))


def _layer_norm(v, g, b):
    mu = jnp.mean(v, axis=-1, keepdims=True)
    xc = v - mu
    var = jnp.mean(xc * xc, axis=-1, keepdims=True)
    return xc * lax.rsqrt(var + LN_EPS) * g + b


def _mod_kernel(c_ref, w_ref, b_ref, o_ref):
    c = c_ref[...]
    s = (c * _sigmoid(c)).astype(BF16)
    o_ref[0] = jnp.dot(s, w_ref[0].astype(BF16), preferred_element_type=F32) + b_ref[0]


def _modulation(cond, ada_w, ada_b):
    depth, d, n = ada_w.shape
    m = cond.shape[0]
    tn = min(n, 1536)
    return pl.pallas_call(
        _mod_kernel,
        grid=(depth, n // tn),
        in_specs=[pl.BlockSpec((m, d), lambda l, j: (0, 0)),
                  pl.BlockSpec((1, d, tn), lambda l, j: (l, 0, j)),
                  pl.BlockSpec((1, 1, tn), lambda l, j: (l, 0, j))],
        out_specs=pl.BlockSpec((1, m, tn), lambda l, j: (l, 0, j)),
        out_shape=jax.ShapeDtypeStruct((depth, m, n), F32),
        compiler_params=_cparams(("arbitrary", "arbitrary")),
        name="adaln_mod",
    )(cond, ada_w, ada_b.reshape(depth, 1, n))


def _s5_disc_kernel(lr_ref, li_ref, ldt_ref, br_ref, bi_ref, ar_ref, ai_ref, bbr_ref, bbi_ref):
    lr, li = lr_ref[...], li_ref[...]
    dt = jnp.exp(ldt_ref[...])
    mag = jnp.exp(lr * dt)
    ar = mag * jnp.cos(li * dt)
    ai = mag * jnp.sin(li * dt)
    den = lr * lr + li * li
    fr = ((ar - 1.0) * lr + ai * li) / den
    fi = (ai * lr - (ar - 1.0) * li) / den
    br, bi = br_ref[...], bi_ref[...]
    ar_ref[...] = ar
    ai_ref[...] = ai
    bbr_ref[...] = fr * br - fi * bi
    bbi_ref[...] = fr * bi + fi * br


def _s5_discretise(lam_re, lam_im, log_dt, b_re, b_im):
    lead = b_re.shape[:-2]
    p, s = b_re.shape[-2:]
    n = math.prod(lead)
    full = lead + (p, s)
    flat = lambda a: jnp.broadcast_to(a, full).reshape(n, p * s)
    args = (flat(lam_re[..., None]), flat(lam_im[..., None]), flat(log_dt[..., None, None]), flat(b_re), flat(b_im))
    rb = 8 if n % 8 == 0 else n
    spec = pl.BlockSpec((rb, p * s), lambda i: (i, 0))
    outs = pl.pallas_call(
        _s5_disc_kernel,
        grid=(n // rb,),
        in_specs=[spec] * 5,
        out_specs=[spec] * 4,
        out_shape=[jax.ShapeDtypeStruct((n, p * s), F32)] * 4,
        compiler_params=_cparams(("arbitrary",)),
        name="s5_discretise",
    )(*args)
    ar, ai, bbr, bbi = (o.reshape(full) for o in outs)
    return ar[..., 0], ai[..., 0], bbr, bbi


def _softplus_neg_kernel(lam_ref, o_ref):
    z = -lam_ref[...]
    o_ref[...] = LRU_C * (jnp.maximum(z, 0.0) + jnp.log(1.0 + jnp.exp(-jnp.abs(z))))


def _softplus_neg(lam2d):
    return pl.pallas_call(
        _softplus_neg_kernel,
        out_shape=jax.ShapeDtypeStruct(lam2d.shape, F32),
        name="lru_softplus",
    )(lam2d)


def _inproj_kernel(x_ref, sc_ref, sh_ref, wa_ref, wb_ref, wga_ref, wgb_ref, xa_ref, xb_ref, ga_ref, gb_ref):
    tb, bb, d = x_ref.shape
    xm = x_ref[...] * (1.0 + sc_ref[0]) + sh_ref[0]
    xm = xm.reshape(tb * bb, d).astype(BF16)
    for w_ref, o_ref in ((wa_ref, xa_ref), (wb_ref, xb_ref), (wga_ref, ga_ref), (wgb_ref, gb_ref)):
        y = jnp.dot(xm, w_ref[...], preferred_element_type=F32)
        o_ref[...] = y.reshape(tb, bb, y.shape[-1]).astype(o_ref.dtype)


def _inproj(xall, sc, sh, w_a, w_b, w_ga, w_gb, n_lat_t):
    ttot, b, d = xall.shape
    tb = TB_IN
    nl = n_lat_t // tb
    ds = w_b.shape[1]
    mod_spec = pl.BlockSpec((1, b, d), lambda i: (jnp.where(i >= nl, 1, 0), 0, 0))
    tok = lambda c: pl.BlockSpec((tb, b, c), lambda i: (i, 0, 0))
    return pl.pallas_call(
        _inproj_kernel,
        grid=(ttot // tb,),
        in_specs=[tok(d), mod_spec, mod_spec,
                  _const_spec(w_a.shape), _const_spec(w_b.shape), _const_spec(w_ga.shape), _const_spec(w_gb.shape)],
        out_specs=[tok(d), tok(ds), tok(d), tok(d)],
        out_shape=[jax.ShapeDtypeStruct((ttot, b, d), F32), jax.ShapeDtypeStruct((ttot, b, ds), BF16),
                   jax.ShapeDtypeStruct((ttot, b, d), BF16), jax.ShapeDtypeStruct((ttot, b, d), BF16)],
        compiler_params=_cparams(("arbitrary",)),
        name="in_proj",
    )(xall, sc, sh, w_a, w_b, w_ga, w_gb)


def _mixer_direction(reverse, seq_start, any_start, xa_ref, xb_ref, cw_ref, cb_ref, wg_ref, ba_ref, bx_ref,
                     sp_ref, bcat_ref, are_ref, aim_ref, ccat_ref, ext_s, a_s, g_s, hl_s, hb_s, hs_s, ys_s):
    tb, bb, d = xa_ref.shape
    rows = tb * bb
    k_w = cw_ref.shape[0]
    halo = k_w - 1
    ds = xb_ref.shape[2]
    n_half = bcat_ref.shape[0]
    hw = ds // n_half
    ns = are_ref.shape[2]
    cw = hb_s.shape[3] // 2
    nt = wg_ref.shape[0]
    tw = d // nt
    x_lo = 0 if reverse else halo
    h_new, h_old = (tb, 0) if reverse else (0, tb)
    order = lambda s: tb - 1 - s if reverse else s

    @pl.when(any_start)
    def _():
        hl_s[...] = jnp.zeros_like(hl_s)
        hs_s[...] = jnp.zeros_like(hs_s)

    @pl.when(seq_start)
    def _():
        ext_s[h_new:h_new + halo] = jnp.zeros((halo, bb, d), F32)

    @pl.when(jnp.logical_not(seq_start))
    def _():
        ext_s[h_new:h_new + halo] = ext_s[h_old:h_old + halo]

    ext_s[x_lo:x_lo + tb] = xa_ref[...]
    u = xb_ref[...].reshape(rows, ds)

    def s5_project(hf, c):
        cols = slice(c * 2 * cw, (c + 1) * 2 * cw)
        bu = jnp.dot(u[:, hf * hw:(hf + 1) * hw], bcat_ref[hf, :, cols], preferred_element_type=F32)
        return bu.reshape(tb, bb, 2 * cw)

    def s5_scan(hf, c, bu, slot):
        lo = c * 2 * cw
        ar = jnp.broadcast_to(are_ref[hf, :, c * cw:(c + 1) * cw], (bb, cw))
        ai = jnp.broadcast_to(aim_ref[hf, :, c * cw:(c + 1) * cw], (bb, cw))
        hr, hi = hs_s[hf, :, lo:lo + cw], hs_s[hf, :, lo + cw:lo + 2 * cw]
        for s in range(tb):
            t = order(s)
            hr, hi = ar * hr - ai * hi + bu[t, :, :cw], ar * hi + ai * hr + bu[t, :, cw:]
            hb_s[slot, t, :, :cw] = hr.astype(BF16)
            hb_s[slot, t, :, cw:] = hi.astype(BF16)
        hs_s[hf, :, lo:lo + cw] = hr
        hs_s[hf, :, lo + cw:lo + 2 * cw] = hi

    def s5_readout(hf, c, slot):
        cols = slice(c * 2 * cw, (c + 1) * 2 * cw)
        part = jnp.dot(hb_s[slot].reshape(rows, 2 * cw), ccat_ref[hf, cols, :], preferred_element_type=F32)
        if c == 0:
            ys_s[:, hf * hw:(hf + 1) * hw] = part
        else:
            ys_s[:, hf * hw:(hf + 1) * hw] += part

    def gate_matmul(q):
        sl = slice(q * tw, (q + 1) * tw)
        xq = cb_ref[:, sl] + cw_ref[k_w - 1:k_w, :, sl] * ext_s[x_lo:x_lo + tb, :, sl]
        for k in range(k_w - 1):
            off = halo - k if reverse else k
            xq = xq + cw_ref[k:k + 1, :, sl] * ext_s[off:off + tb, :, sl]
        xq = xq.reshape(rows, tw)
        return q, xq, jnp.dot(xq.astype(BF16), wg_ref[q], preferred_element_type=F32)

    def gate_coefficients(q, xq, z):
        sl = slice(q * tw, (q + 1) * tw)
        r = _sigmoid(z[:, :tw] + ba_ref[:, sl])
        gi = _sigmoid(z[:, tw:] + bx_ref[:, sl])
        nla = r * sp_ref[:, sl]
        a = jnp.exp(-nla)
        gx = jnp.sqrt(jnp.tanh(nla) * (a * a + 1.0)) * (gi * xq)
        a_s[:, :, sl] = a.reshape(tb, bb, tw)
        g_s[:, :, sl] = gx.reshape(tb, bb, tw)

    pieces = [(hf, c) for hf in range(n_half) for c in range(ns // cw)]
    tiles_started = 0
    bu_next = s5_project(*pieces[0])
    pending = None
    for n, (hf, c) in enumerate(pieces):
        bu = bu_next
        if n + 1 < len(pieces):
            bu_next = s5_project(*pieces[n + 1])
        started = None
        if tiles_started < nt and n * nt >= tiles_started * len(pieces):
            started = gate_matmul(tiles_started)
            tiles_started += 1
        s5_scan(hf, c, bu, n % 2)
        s5_readout(hf, c, n % 2)
        if pending is not None:
            gate_coefficients(*pending)
        pending = started
    if pending is not None:
        gate_coefficients(*pending)
    for q in range(tiles_started, nt):
        gate_coefficients(*gate_matmul(q))

    cl = min(LRU_CHUNK, d)
    for c0 in range(0, d, cl):
        cs = slice(c0, c0 + cl)
        h = hl_s[:, cs]
        for s in range(tb):
            t = order(s)
            h = a_s[t, :, cs] * h + g_s[t, :, cs]
            g_s[t, :, cs] = h
        hl_s[:, cs] = h


def _mixer_rev_kernel(xa_ref, xb_ref, cw_ref, cb_ref, wg_ref, ba_ref, bx_ref, sp_ref, bcat_ref, are_ref,
                      aim_ref, ccat_ref, ybl_ref, ybs_ref, ext_s, a_s, g_s, hl_s, hb_s, hs_s, ys_s, *, nc):
    i = pl.program_id(0)
    tb, bb, _ = xa_ref.shape
    _mixer_direction(True, jnp.logical_or(i == 0, i == nc), i == 0, xa_ref, xb_ref, cw_ref, cb_ref, wg_ref,
                     ba_ref, bx_ref, sp_ref, bcat_ref, are_ref, aim_ref, ccat_ref, ext_s, a_s, g_s, hl_s, hb_s,
                     hs_s, ys_s)
    ybl_ref[...] = g_s[...].astype(ybl_ref.dtype)
    ybs_ref[...] = ys_s[...].reshape(tb, bb, ys_s.shape[-1]).astype(ybs_ref.dtype)


def _mixer_fwd_kernel(xa_ref, xb_ref, cw_ref, cb_ref, wg_ref, ba_ref, bx_ref, sp_ref, bcat_ref, are_ref,
                      aim_ref, ccat_ref, ybl_ref, ybs_ref, ga_ref, gb_ref, x_ref, g1_ref, dsk_ref, gluw_ref,
                      glub_ref, pa_ref, pb_ref, wo_ref, lng_ref, lnb_ref, o_ref,
                      ext_s, a_s, g_s, hl_s, hb_s, hs_s, ys_s, *, nc, ctx_out, alpha):
    i = pl.program_id(0)
    tb, bb, d = xa_ref.shape
    rows = tb * bb
    _mixer_direction(False, jnp.logical_or(i == 0, i == nc), i == 0, xa_ref, xb_ref, cw_ref, cb_ref, wg_ref,
                     ba_ref, bx_ref, sp_ref, bcat_ref, are_ref, aim_ref, ccat_ref, ext_s, a_s, g_s, hl_s, hb_s,
                     hs_s, ys_s)

    def merge():
        ds = xb_ref.shape[2]
        ya = (g_s[...] + ybl_ref[...].astype(F32)).reshape(rows, d).astype(BF16)
        ta = jnp.dot(ya, pa_ref[...], preferred_element_type=F32)
        u = xb_ref[...].reshape(rows, ds).astype(F32)
        yb = ys_s[...] + ybs_ref[...].reshape(rows, ds).astype(F32) + dsk_ref[...] * u
        z = _gelu_tanh(yb)
        zg = jnp.dot(z.astype(BF16), gluw_ref[...], preferred_element_type=F32) + glub_ref[...]
        ma = _sigmoid(ga_ref[...].reshape(rows, d).astype(F32)) * ta
        yb = (z * _sigmoid(zg)).astype(BF16)
        tbp = jnp.dot(yb, pb_ref[...], preferred_element_type=F32)
        m = ma + _sigmoid(gb_ref[...].reshape(rows, d).astype(F32)) * tbp
        out = jnp.dot(m.astype(BF16), wo_ref[...], preferred_element_type=F32).reshape(tb, bb, d)
        v = alpha * x_ref[...] + (1.0 + g1_ref[0]) * out
        o_ref[...] = _layer_norm(v, lng_ref[...], lnb_ref[...])

    if ctx_out:
        merge()
    else:
        pl.when(i >= nc)(merge)


def _mixer(xa, xb, ga, gb, xall, g1, lp, n_lat_t, ctx_out, alpha):
    ttot, b, d = xa.shape
    ds = xb.shape[2]
    tb = TB_MIX
    nblk = ttot // tb
    nl = n_lat_t // tb
    nc = nblk - nl
    ns2 = lp["bcat"].shape[-1]
    n_half = lp["bcat"].shape[1]

    scratch = [pltpu.VMEM((tb + lp["conv_w"].shape[1] - 1, b, d), F32),
               pltpu.VMEM((tb, b, d), F32),
               pltpu.VMEM((tb, b, d), F32),
               pltpu.VMEM((b, d), F32),
               pltpu.VMEM((2, tb, b, 2 * min(S5_CHUNK, ns2 // 2)), BF16),
               pltpu.VMEM((n_half, b, ns2), F32),
               pltpu.VMEM((tb * b, ds), F32)]

    def dir_args(dr):
        return (lp["conv_w"][dr], lp["conv_b"][dr], lp["wg"][dr], lp["ba"][dr], lp["bx"][dr], lp["sp"][dr],
                lp["bcat"][dr], lp["are"][dr], lp["aim"][dr], lp["ccat"][dr])

    def dir_specs(args):
        return [_const_spec(a.shape) for a in args]

    rev = lambda i: (nblk - 1 - i, 0, 0)
    tok_rev = lambda c: pl.BlockSpec((tb, b, c), rev)
    args = dir_args(1)
    ybl, ybs = pl.pallas_call(
        functools.partial(_mixer_rev_kernel, nc=nc),
        grid=(nblk,),
        in_specs=[tok_rev(d), tok_rev(ds)] + dir_specs(args),
        out_specs=[tok_rev(d), tok_rev(ds)],
        out_shape=[jax.ShapeDtypeStruct((ttot, b, d), BF16), jax.ShapeDtypeStruct((ttot, b, ds), BF16)],
        scratch_shapes=scratch,
        compiler_params=_cparams(("arbitrary",)),
        name="mixer_rev",
    )(xa, xb, *args)

    fwd = lambda i: (jnp.where(i < nc, nl + i, i - nc), 0, 0)
    tok_fwd = lambda c: pl.BlockSpec((tb, b, c), fwd)
    if ctx_out:
        out_spec, t_out = tok_fwd(d), ttot
    else:
        out_spec, t_out = pl.BlockSpec((tb, b, d), lambda i: (jnp.maximum(i - nc, 0), 0, 0)), n_lat_t
    g1_spec = pl.BlockSpec((1, b, d), lambda i: (jnp.where(i < nc, 1, 0), 0, 0))
    args = dir_args(0)
    tail = (lp["d_skip"], lp["glu_w"], lp["glu_b"], lp["p_a"], lp["p_b"], lp["w_out"], lp["ln1_g"], lp["ln1_b"])
    return pl.pallas_call(
        functools.partial(_mixer_fwd_kernel, nc=nc, ctx_out=ctx_out, alpha=alpha),
        grid=(nblk,),
        in_specs=([tok_fwd(d), tok_fwd(ds)] + dir_specs(args)
                  + [tok_fwd(d), tok_fwd(ds), tok_fwd(d), tok_fwd(d), tok_fwd(d), g1_spec] + dir_specs(tail)),
        out_specs=out_spec,
        out_shape=jax.ShapeDtypeStruct((t_out, b, d), F32),
        scratch_shapes=scratch,
        compiler_params=_cparams(("arbitrary",)),
        name="mixer_fwd",
    )(xa, xb, *args, ybl, ybs, ga, gb, xall, g1, *tail)


def _ffn_kernel(*refs, piece, vertical, n_row_blocks, alpha):
    if vertical:
        (x_ref, xt_ref, xbm_ref, sc_ref, sh_ref, g2_ref, wg_ref, wv_ref, wd_ref, cw_ref, cb_ref, lng_ref, lnb_ref,
         o_ref, u_s, gs_s) = refs[-16:]
    else:
        (x_ref, sc_ref, sh_ref, g2_ref, wg_ref, wv_ref, wd_ref, cw_ref, cb_ref, lng_ref, lnb_ref,
         o_ref, u_s, gs_s) = refs[-14:]
    j, k = pl.program_id(1), pl.program_id(2)
    nt, bb, d = x_ref.shape
    ck = wg_ref.shape[1]
    n_pre = 1 if vertical else 0
    n_main = nt // piece
    n_ext = n_main + 2 * n_pre
    vo = n_pre * piece
    stride = piece + 1 if vertical else piece

    @pl.when(k == 0)
    def _():
        mod = lambda r: (r[...] * (1.0 + sc_ref[0]) + sh_ref[0]).astype(BF16)
        u_s[vo:vo + nt] = mod(x_ref)
        if vertical:
            u_s[0:vo] = mod(xt_ref)
            u_s[vo + nt:vo + nt + vo] = mod(xbm_ref)
        o_ref[...] = jnp.zeros_like(o_ref)

    for p in (range(n_ext + 1) if vertical else (0,)):
        gs_s[p * stride] = jnp.zeros((bb, ck), F32)
    if not vertical:
        gs_s[1 + nt] = jnp.zeros((bb, ck), F32)

    def gate_piece(e):
        g = jnp.dot(u_s[e * piece:(e + 1) * piece].reshape(piece * bb, d), wg_ref[...],
                    preferred_element_type=F32)
        if vertical and e == 0:
            g = jnp.where(j == 0, 0.0, g)
        if vertical and e == n_ext - 1:
            g = jnp.where(j == n_row_blocks - 1, 0.0, g)
        gs_s[1 + e * stride:1 + e * stride + piece] = g.reshape(piece, bb, ck)

    drs = (-1, 0, 1) if vertical else (0,)
    gates_done = 0
    for r in range(n_main):
        need = min(r + n_pre + 1, n_ext - 1)
        while gates_done <= need:
            gate_piece(gates_done)
            gates_done += 1
        val = jnp.dot(u_s[vo + r * piece:vo + (r + 1) * piece].reshape(piece * bb, d), wv_ref[...],
                      preferred_element_type=F32)
        conv = None
        for dr in drs:
            for dc in (-1, 0, 1):
                s0 = 1 + (r + n_pre + dr) * stride + dc
                tap = (dr + 1) * 3 + dc + 1
                term = cw_ref[tap:tap + 1] * gs_s[s0:s0 + piece]
                conv = term if conv is None else conv + term
        act = _gelu_tanh(conv.reshape(piece * bb, ck) + cb_ref[...]) * val
        part = jnp.dot(act.astype(BF16), wd_ref[...], preferred_element_type=F32)
        o_ref[r * piece:(r + 1) * piece] += part.reshape(piece, bb, d)

    @pl.when(k == pl.num_programs(2) - 1)
    def _():
        v = alpha * x_ref[...] + (1.0 + g2_ref[0]) * o_ref[...]
        o_ref[...] = _layer_norm(v, lng_ref[...], lnb_ref[...])


def _conv_ffn(xall, sc, sh, g2, lp, n_lat_t, ctx_out, alpha):
    ttot, b, d = xall.shape
    dff = lp["ffn_down"].shape[0]
    ck = min(FFN_CK, dff)
    nk = dff // ck
    bb = min(FFN_BB, b)
    tc = ttot - n_lat_t
    w_up, w_down = lp["ffn_up"], lp["ffn_down"]

    wspecs = [pl.BlockSpec((d, ck), lambda ib, j, k: (0, k)),
              pl.BlockSpec((d, ck), lambda ib, j, k: (0, nk + k)),
              pl.BlockSpec((ck, d), lambda ib, j, k: (k, 0)),
              pl.BlockSpec((9, 1, ck), lambda ib, j, k: (0, 0, k)),
              pl.BlockSpec((1, ck), lambda ib, j, k: (0, k)),
              pl.BlockSpec((1, d), lambda ib, j, k: (0, 0)),
              pl.BlockSpec((1, d), lambda ib, j, k: (0, 0))]
    wargs = (w_up, w_up, w_down, lp["ffn_conv_w"], lp["ffn_conv_b"], lp["ln2_g"], lp["ln2_b"])

    w = GRID_W
    n_img_rows = n_lat_t // w
    r = min(FFN_ROWS, n_img_rows)
    nt = r * w
    nrb = n_img_rows // r
    mod_lat = pl.BlockSpec((1, bb, d), lambda ib, j, k: (0, ib, 0))
    t_out = ttot if ctx_out else n_lat_t
    out = pl.pallas_call(
        functools.partial(_ffn_kernel, piece=w, vertical=True, n_row_blocks=nrb, alpha=alpha),
        grid=(b // bb, nrb, nk),
        in_specs=[pl.BlockSpec((nt, bb, d), lambda ib, j, k: (j, ib, 0), pipeline_mode=pl.Buffered(1)),
                  pl.BlockSpec((w, bb, d), lambda ib, j, k: (jnp.maximum(j * r - 1, 0), ib, 0),
                               pipeline_mode=pl.Buffered(1)),
                  pl.BlockSpec((w, bb, d), lambda ib, j, k: (jnp.minimum(j * r + r, n_img_rows - 1), ib, 0),
                               pipeline_mode=pl.Buffered(1)),
                  mod_lat, mod_lat, mod_lat] + wspecs,
        out_specs=pl.BlockSpec((nt, bb, d), lambda ib, j, k: (j, ib, 0)),
        out_shape=jax.ShapeDtypeStruct((t_out, b, d), F32),
        scratch_shapes=[pltpu.VMEM((nt + 2 * w, bb, d), BF16),
                        pltpu.VMEM(((r + 2) * (w + 1) + 1, bb, ck), F32)],
        compiler_params=_cparams(("arbitrary", "arbitrary", "arbitrary")),
        name="conv_ffn_latent",
    )(xall, xall, xall, sc, sh, g2, *wargs)
    if not ctx_out:
        return out

    cblk = n_lat_t // tc
    mod_ctx = pl.BlockSpec((1, bb, d), lambda ib, j, k: (1, ib, 0))
    return pl.pallas_call(
        functools.partial(_ffn_kernel, piece=min(w, tc), vertical=False, n_row_blocks=1, alpha=alpha),
        grid=(b // bb, 1, nk),
        in_specs=[pl.BlockSpec(memory_space=pl.ANY),
                  pl.BlockSpec((tc, bb, d), lambda ib, j, k: (cblk, ib, 0), pipeline_mode=pl.Buffered(1)),
                  mod_ctx, mod_ctx, mod_ctx] + wspecs,
        out_specs=pl.BlockSpec((tc, bb, d), lambda ib, j, k: (cblk, ib, 0)),
        out_shape=jax.ShapeDtypeStruct((ttot, b, d), F32),
        scratch_shapes=[pltpu.VMEM((tc, bb, d), BF16),
                        pltpu.VMEM((tc + 2, bb, ck), F32)],
        input_output_aliases={0: 0},
        compiler_params=_cparams(("arbitrary", "arbitrary", "arbitrary")),
        name="conv_ffn_context",
    )(out, xall, sc, sh, g2, *wargs)


def _pack_gate_weights(wa, wx):
    h, hd = wa.shape[-3], wa.shape[-1]
    d = h * hd
    tile = min(MXU_TILE, d)
    hpt = tile // hd
    lead = wa.shape[:-3]
    eye = jnp.eye(hpt, dtype=wa.dtype)

    def dense(w):
        w = w.reshape(lead + (d // tile, hpt, hd, hd))
        return jnp.einsum('...qhij,hg->...qhigj', w, eye).reshape(lead + (d // tile, tile, tile))

    return jnp.concatenate([dense(wa), dense(wx)], axis=-1).astype(BF16)


def _pack_s5(ar, ai, bbr, bbi, c_re, c_im):
    lead = bbr.shape[:-3]
    g, p, s = bbr.shape[-3:]
    gpt = min(MXU_TILE // s, g)
    nh = g // gpt
    eye = jnp.eye(gpt, dtype=F32)

    def b_dense(bb_):
        bb_ = bb_.reshape(lead + (nh, gpt, p, s))
        return jnp.einsum('...hgpc,gk->...hgckp', bb_, eye).reshape(lead + (nh, gpt * s, gpt * p))

    def c_dense(cc):
        cc = cc.reshape(lead + (nh, gpt, s, p))
        return jnp.einsum('...hgcp,gk->...hgpkc', cc, eye).reshape(lead + (nh, gpt * p, gpt * s))

    ns = gpt * p
    cw = min(S5_CHUNK, ns)
    nchunk = ns // cw
    b_r = b_dense(bbr).reshape(lead + (nh, gpt * s, nchunk, 1, cw))
    b_i = b_dense(bbi).reshape(lead + (nh, gpt * s, nchunk, 1, cw))
    bcat = jnp.concatenate([b_r, b_i], axis=-2).reshape(lead + (nh, gpt * s, 2 * ns)).astype(BF16)
    c_r = c_dense(c_re).reshape(lead + (nh, nchunk, 1, cw, gpt * s))
    c_i = c_dense(-c_im).reshape(lead + (nh, nchunk, 1, cw, gpt * s))
    ccat = jnp.concatenate([c_r, c_i], axis=-3).reshape(lead + (nh, 2 * ns, gpt * s)).astype(BF16)
    are = ar.reshape(lead + (nh, 1, gpt * p))
    aim = ai.reshape(lead + (nh, 1, gpt * p))
    return bcat, are, aim, ccat


def kernel(x, c, ctx, c_ctx, ada_w, ada_b, w_in, lru_conv_w, lru_conv_b, lru_wa, lru_ba, lru_wx, lru_bx, lru_lam, s5_lam_re, s5_lam_im, s5_log_dt, s5_b_re, s5_b_im, s5_c_re, s5_c_im, s5_d, s5_glu_w, s5_glu_b, p_a, p_b, w_out, ln1_g, ln1_b, ffn_up, ffn_conv_w, ffn_conv_b, ffn_down, ln2_g, ln2_b):
    bsz, seq, d = x.shape
    n_ctx = ctx.shape[1]
    depth = ada_w.shape[0]
    d_lru = lru_lam.shape[-1]
    d_s5 = s5_d.shape[-1]
    dff = ffn_down.shape[1]
    alpha = (2.0 * depth) ** 0.25
    assert seq % n_ctx == 0 and seq % GRID_W == 0 and n_ctx % TB_MIX == 0 and n_ctx % TB_IN == 0

    m_rows = -(-(bsz + 1) // 8) * 8
    cond = jnp.zeros((m_rows, d), F32).at[:bsz].set(c).at[bsz].set(c_ctx)
    modv = _modulation(cond, ada_w, ada_b).reshape(depth, m_rows, 6, d)

    ar, ai, bbr, bbi = _s5_discretise(s5_lam_re, s5_lam_im, s5_log_dt, s5_b_re, s5_b_im)
    bcat, are, aim, ccat = _pack_s5(ar, ai, bbr, bbi, s5_c_re, s5_c_im)
    sp = _softplus_neg(lru_lam.reshape(depth * 2, d_lru)).reshape(depth, 2, 1, d_lru)
    wg = _pack_gate_weights(lru_wa, lru_wx)
    w_in16 = w_in.astype(BF16)
    row = lambda a: a[..., None, :]

    xall = jnp.concatenate([jnp.transpose(x, (1, 0, 2)), jnp.transpose(ctx, (1, 0, 2))], axis=0)
    for l in range(depth):
        ctx_out = l < depth - 1
        lat = modv[l, :bsz]
        cx = jnp.broadcast_to(modv[l, bsz][None], (bsz, 6, d))
        mods = jnp.stack([lat, cx], axis=0)
        sh1, sc1, g1, sh2, sc2, g2 = (mods[:, :, n] for n in range(6))
        lp = dict(conv_w=lru_conv_w[l][:, :, None, :], conv_b=row(lru_conv_b[l]), wg=wg[l], ba=row(lru_ba[l]),
                  bx=row(lru_bx[l]), sp=sp[l], bcat=bcat[l], are=are[l], aim=aim[l], ccat=ccat[l],
                  d_skip=row(s5_d[l]), glu_w=s5_glu_w[l].astype(BF16), glu_b=row(s5_glu_b[l]),
                  p_a=p_a[l].astype(BF16), p_b=p_b[l].astype(BF16), w_out=w_out[l].astype(BF16),
                  ln1_g=row(ln1_g[l]), ln1_b=row(ln1_b[l]),
                  ffn_up=ffn_up[l].astype(BF16), ffn_down=ffn_down[l].astype(BF16),
                  ffn_conv_w=ffn_conv_w[l].reshape(9, 1, dff), ffn_conv_b=row(ffn_conv_b[l]),
                  ln2_g=row(ln2_g[l]), ln2_b=row(ln2_b[l]))
        wl = w_in16[l]
        xa, xb, ga, gb = _inproj(xall, sc1, sh1, wl[:, :d_lru], wl[:, d_lru:d_lru + d_s5],
                                 wl[:, d_lru + d_s5:d_lru + d_s5 + d], wl[:, d_lru + d_s5 + d:], seq)
        xmid =_mixer(xa, xb, ga, gb, xall, g1, lp, seq, ctx_out, alpha)
        xall = _conv_ffn(xmid, sc2, sh2, g2, lp, seq, ctx_out, alpha)
    return jnp.transpose(xall[:seq], (1, 0, 2))
```

```python
import functools
import math

import jax
import jax.numpy as jnp
from jax import lax
from jax.experimental import pallas as pl
from jax.experimental.pallas import tpu as pltpu

F32 = jnp.float32
BF16 = jnp.bfloat16

GRID_W = 64
LRU_C = 8.0
LN_EPS = 1e-5
S5_GROUP = 16
MXU_TILE = 256
VMEM_LIMIT = 60 * 1024 * 1024

TB_MIX = 16
FFN_ROWS = 4
FFN_BB = 8
FFN_CK = 512
LRU_CHUNK = 512
S5_CHUNK = 256


def _cparams(sem):
    return pltpu.CompilerParams(dimension_semantics=sem, vmem_limit_bytes=VMEM_LIMIT)


def _const_spec(shape):
    nd = len(shape)
    return pl.BlockSpec(shape, lambda *_: (0,) * nd, pipeline_mode=pl.Buffered(1))


def _gelu_tanh(x):
    return 0.5 * x * (1.0 + jnp.tanh(math.sqrt(2.0 / math.pi) * (x + 0.044715 * (x * x * x))))


def _sigmoid(x):
    return 1.0 / (1.0 + jnp.exp(-x))


def _layer_norm(v, g, b):
    mu = jnp.mean(v, axis=-1, keepdims=True)
    xc = v - mu
    var = jnp.mean(xc * xc, axis=-1, keepdims=True)
    return xc * lax.rsqrt(var + LN_EPS) * g + b


def _mod_kernel(c_ref, w_ref, b_ref, o_ref):
    c = c_ref[...]
    s = (c * _sigmoid(c)).astype(BF16)
    o_ref[0] = jnp.dot(s, w_ref[0].astype(BF16), preferred_element_type=F32) + b_ref[0]


def _modulation(cond, ada_w, ada_b):
    depth, d, n = ada_w.shape
    m = cond.shape[0]
    tn = min(n, 1536)
    return pl.pallas_call(
        _mod_kernel,
        grid=(depth, n // tn),
        in_specs=[pl.BlockSpec((m, d), lambda l, j: (0, 0)),
                  pl.BlockSpec((1, d, tn), lambda l, j: (l, 0, j)),
                  pl.BlockSpec((1, 1, tn), lambda l, j: (l, 0, j))],
        out_specs=pl.BlockSpec((1, m, tn), lambda l, j: (l, 0, j)),
        out_shape=jax.ShapeDtypeStruct((depth, m, n), F32),
        compiler_params=_cparams(("arbitrary", "arbitrary")),
        name="adaln_mod",
    )(cond, ada_w, ada_b.reshape(depth, 1, n))


def _s5_disc_kernel(lr_ref, li_ref, ldt_ref, br_ref, bi_ref, ar_ref, ai_ref, bbr_ref, bbi_ref):
    lr, li = lr_ref[...], li_ref[...]
    dt = jnp.exp(ldt_ref[...])
    mag = jnp.exp(lr * dt)
    ar = mag * jnp.cos(li * dt)
    ai = mag * jnp.sin(li * dt)
    den = lr * lr + li * li
    fr = ((ar - 1.0) * lr + ai * li) / den
    fi = (ai * lr - (ar - 1.0) * li) / den
    br, bi = br_ref[...], bi_ref[...]
    ar_ref[...] = ar
    ai_ref[...] = ai
    bbr_ref[...] = fr * br - fi * bi
    bbi_ref[...] = fr * bi + fi * br


def _s5_discretise(lam_re, lam_im, log_dt, b_re, b_im):
    lead = b_re.shape[:-2]
    p, s = b_re.shape[-2:]
    n = math.prod(lead)
    full = lead + (p, s)
    flat = lambda a: jnp.broadcast_to(a, full).reshape(n, p * s)
    args = (flat(lam_re[..., None]), flat(lam_im[..., None]), flat(log_dt[..., None, None]), flat(b_re), flat(b_im))
    rb = 8 if n % 8 == 0 else n
    spec = pl.BlockSpec((rb, p * s), lambda i: (i, 0))
    outs = pl.pallas_call(
        _s5_disc_kernel,
        grid=(n // rb,),
        in_specs=[spec] * 5,
        out_specs=[spec] * 4,
        out_shape=[jax.ShapeDtypeStruct((n, p * s), F32)] * 4,
        compiler_params=_cparams(("arbitrary",)),
        name="s5_discretise",
    )(*args)
    ar, ai, bbr, bbi = (o.reshape(full) for o in outs)
    return ar[..., 0], ai[..., 0], bbr, bbi


def _softplus_neg_kernel(lam_ref, o_ref):
    z = -lam_ref[...]
    o_ref[...] = LRU_C * (jnp.maximum(z, 0.0) + jnp.log(1.0 + jnp.exp(-jnp.abs(z))))


def _softplus_neg(lam2d):
    return pl.pallas_call(
        _softplus_neg_kernel,
        out_shape=jax.ShapeDtypeStruct(lam2d.shape, F32),
        name="lru_softplus",
    )(lam2d)


def _mixer_direction(reverse, seq_start, any_start, xa_ref, xb_ref, cw_ref, cb_ref, wg_ref, ba_ref, bx_ref,
                     sp_ref, bcat_ref, are_ref, aim_ref, ccat_ref, ext_s, a_s, g_s, hl_s, hb_s, hs_s, ys_s,
                     inproj=None):
    tb, bb, d = xa_ref.shape
    rows = tb * bb
    k_w = cw_ref.shape[0]
    halo = k_w - 1
    ds = xb_ref.shape[2]
    n_half = bcat_ref.shape[0]
    hw = ds // n_half
    ns = are_ref.shape[2]
    cw = hb_s.shape[3] // 2
    nt = wg_ref.shape[0]
    tw = d // nt
    x_lo = 0 if reverse else halo
    h_new, h_old = (tb, 0) if reverse else (0, tb)
    order = lambda s: tb - 1 - s if reverse else s

    @pl.when(any_start)
    def _():
        hl_s[...] = jnp.zeros_like(hl_s)
        hs_s[...] = jnp.zeros_like(hs_s)

    @pl.when(seq_start)
    def _():
        ext_s[h_new:h_new + halo] = jnp.zeros((halo, bb, d), F32)

    @pl.when(jnp.logical_not(seq_start))
    def _():
        ext_s[h_new:h_new + halo] = ext_s[h_old:h_old + halo]

    fillers = []
    u_half = {}
    if inproj is None:
        ext_s[x_lo:x_lo + tb] = xa_ref[...]
        u = xb_ref[...].reshape(rows, ds)
        for hf in range(n_half):
            u_half[hf] = u[:, hf * hw:(hf + 1) * hw]
    else:
        x_ref, sc_ref, sh_ref, wa_ref, wb_ref, wga_ref, wgb_ref, ga_ref, gb_ref = inproj
        xm = (x_ref[...] * (1.0 + sc_ref[0]) + sh_ref[0]).reshape(rows, d).astype(BF16)

        def project_xb(hf):
            v = jnp.dot(xm, wb_ref[:, hf * hw:(hf + 1) * hw], preferred_element_type=F32).astype(BF16)
            xb_ref[:, :, hf * hw:(hf + 1) * hw] = v.reshape(tb, bb, hw)
            u_half[hf] = v

        def project_xa(q):
            sl = slice(q * tw, (q + 1) * tw)
            v = jnp.dot(xm, wa_ref[:, sl], preferred_element_type=F32).reshape(tb, bb, tw)
            xa_ref[:, :, sl] = v
            ext_s[x_lo:x_lo + tb, :, sl] = v

        def project_gate(w_ref, o_ref, q):
            sl = slice(q * tw, (q + 1) * tw)
            v = jnp.dot(xm, w_ref[:, sl], preferred_element_type=F32)
            o_ref[:, :, sl] = v.reshape(tb, bb, tw).astype(o_ref.dtype)

        project_xb(0)
        project_xa(0)
        fillers = ([functools.partial(project_xb, hf) for hf in range(1, n_half)]
                   + [functools.partial(project_xa, q) for q in range(1, nt)]
                   + [functools.partial(project_gate, w, o, q)
                      for w, o in ((wga_ref, ga_ref), (wgb_ref, gb_ref)) for q in range(nt)])

    def s5_project(hf, c):
        cols = slice(c * 2 * cw, (c + 1) * 2 * cw)
        bu = jnp.dot(u_half[hf], bcat_ref[hf, :, cols], preferred_element_type=F32)
        return bu.reshape(tb, bb, 2 * cw)

    def s5_scan(hf, c, bu, slot):
        lo = c * 2 * cw
        ar = jnp.broadcast_to(are_ref[hf, :, c * cw:(c + 1) * cw], (bb, cw))
        ai = jnp.broadcast_to(aim_ref[hf, :, c * cw:(c + 1) * cw], (bb, cw))
        hr, hi = hs_s[hf, :, lo:lo + cw], hs_s[hf, :, lo + cw:lo + 2 * cw]
        for s in range(tb):
            t = order(s)
            hr, hi = ar * hr - ai * hi + bu[t, :, :cw], ar * hi + ai * hr + bu[t, :, cw:]
            hb_s[slot, t, :, :cw] = hr.astype(BF16)
            hb_s[slot, t, :, cw:] = hi.astype(BF16)
        hs_s[hf, :, lo:lo + cw] = hr
        hs_s[hf, :, lo + cw:lo + 2 * cw] = hi

    def s5_readout(hf, c, slot):
        cols = slice(c * 2 * cw, (c + 1) * 2 * cw)
        part = jnp.dot(hb_s[slot].reshape(rows, 2 * cw), ccat_ref[hf, cols, :], preferred_element_type=F32)
        if c == 0:
            ys_s[:, hf * hw:(hf + 1) * hw] = part
        else:
            ys_s[:, hf * hw:(hf + 1) * hw] += part

    def gate_matmul(q):
        sl = slice(q * tw, (q + 1) * tw)
        xq = cb_ref[:, sl] + cw_ref[k_w - 1:k_w, :, sl] * ext_s[x_lo:x_lo + tb, :, sl]
        for k in range(k_w - 1):
            off = halo - k if reverse else k
            xq = xq + cw_ref[k:k + 1, :, sl] * ext_s[off:off + tb, :, sl]
        xq = xq.reshape(rows, tw)
        return q, xq, jnp.dot(xq.astype(BF16), wg_ref[q], preferred_element_type=F32)

    def gate_coefficients(q, xq, z):
        sl = slice(q * tw, (q + 1) * tw)
        r = _sigmoid(z[:, :tw] + ba_ref[:, sl])
        gi = _sigmoid(z[:, tw:] + bx_ref[:, sl])
        nla = r * sp_ref[:, sl]
        a = jnp.exp(-nla)
        gx = jnp.sqrt(jnp.tanh(nla) * (a * a + 1.0)) * (gi * xq)
        a_s[:, :, sl] = a.reshape(tb, bb, tw)
        g_s[:, :, sl] = gx.reshape(tb, bb, tw)

    pieces = [(hf, c) for hf in range(n_half) for c in range(ns // cw)]
    tiles_started = 0
    bu_next = s5_project(*pieces[0])
    pending = None
    per_piece = -(-len(fillers) // max(1, len(pieces) - 1))
    for n, (hf, c) in enumerate(pieces):
        bu = bu_next
        for _ in range(min(per_piece, len(fillers))):
            fillers.pop(0)()
        if n + 1 < len(pieces):
            bu_next = s5_project(*pieces[n + 1])
        started = None
        if tiles_started < nt and n * nt >= tiles_started * len(pieces):
            started = gate_matmul(tiles_started)
            tiles_started += 1
        s5_scan(hf, c, bu, n % 2)
        s5_readout(hf, c, n % 2)
        if pending is not None:
            gate_coefficients(*pending)
        pending = started
    if pending is not None:
        gate_coefficients(*pending)
    while fillers:
        fillers.pop(0)()
    for q in range(tiles_started, nt):
        gate_coefficients(*gate_matmul(q))

    cl = min(LRU_CHUNK, d)
    for c0 in range(0, d, cl):
        cs = slice(c0, c0 + cl)
        h = hl_s[:, cs]
        for s in range(tb):
            t = order(s)
            h = a_s[t, :, cs] * h + g_s[t, :, cs]
            g_s[t, :, cs] = h
        hl_s[:, cs] = h


def _mixer_rev_kernel(x_ref, sc_ref, sh_ref, wa_ref, wb_ref, wga_ref, wgb_ref, cw_ref, cb_ref, wg_ref, ba_ref,
                      bx_ref, sp_ref, bcat_ref, are_ref, aim_ref, ccat_ref, ybl_ref, ybs_ref, xa_ref, xb_ref, ga_ref,
                      gb_ref, ext_s, a_s, g_s, hl_s, hb_s, hs_s, ys_s, *, nc):
    i = pl.program_id(0)
    tb, bb, _ = xa_ref.shape
    _mixer_direction(True, jnp.logical_or(i == 0, i == nc), i == 0, xa_ref, xb_ref, cw_ref, cb_ref, wg_ref,
                     ba_ref, bx_ref, sp_ref, bcat_ref, are_ref, aim_ref, ccat_ref, ext_s, a_s, g_s, hl_s, hb_s,
                     hs_s, ys_s,
                     inproj=(x_ref, sc_ref, sh_ref, wa_ref, wb_ref, wga_ref, wgb_ref, ga_ref, gb_ref))
    ybl_ref[...] = g_s[...].astype(ybl_ref.dtype)
    ybs_ref[...] = ys_s[...].reshape(tb, bb, ys_s.shape[-1]).astype(ybs_ref.dtype)


def _mixer_fwd_kernel(xa_ref, xb_ref, cw_ref, cb_ref, wg_ref, ba_ref, bx_ref, sp_ref, bcat_ref, are_ref,
                      aim_ref, ccat_ref, ybl_ref, ybs_ref, ga_ref, gb_ref, x_ref, g1_ref, dsk_ref, gluw_ref,
                      glub_ref, pa_ref, pb_ref, wo_ref, lng_ref, lnb_ref, o_ref,
                      ext_s, a_s, g_s, hl_s, hb_s, hs_s, ys_s, *, nc, ctx_out, alpha):
    i = pl.program_id(0)
    tb, bb, d = xa_ref.shape
    rows = tb * bb
    _mixer_direction(False, jnp.logical_or(i == 0, i == nc), i == 0, xa_ref, xb_ref, cw_ref, cb_ref, wg_ref,
                     ba_ref, bx_ref, sp_ref, bcat_ref, are_ref, aim_ref, ccat_ref, ext_s, a_s, g_s, hl_s, hb_s,
                     hs_s, ys_s)

    def merge():
        ds = xb_ref.shape[2]
        ya = (g_s[...] + ybl_ref[...].astype(F32)).reshape(rows, d).astype(BF16)
        ta = jnp.dot(ya, pa_ref[...], preferred_element_type=F32)
        u = xb_ref[...].reshape(rows, ds).astype(F32)
        yb = ys_s[...] + ybs_ref[...].reshape(rows, ds).astype(F32) + dsk_ref[...] * u
        z = _gelu_tanh(yb)
        zg = jnp.dot(z.astype(BF16), gluw_ref[...], preferred_element_type=F32) + glub_ref[...]
        ma = _sigmoid(ga_ref[...].reshape(rows, d).astype(F32)) * ta
        yb = (z * _sigmoid(zg)).astype(BF16)
        tbp = jnp.dot(yb, pb_ref[...], preferred_element_type=F32)
        m = ma + _sigmoid(gb_ref[...].reshape(rows, d).astype(F32)) * tbp
        out = jnp.dot(m.astype(BF16), wo_ref[...], preferred_element_type=F32).reshape(tb, bb, d)
        v = alpha * x_ref[...] + (1.0 + g1_ref[0]) * out
        o_ref[...] = _layer_norm(v, lng_ref[...], lnb_ref[...])

    if ctx_out:
        merge()
    else:
        pl.when(i >= nc)(merge)


def _mixer(xall, sc1, sh1, g1, w_parts, lp, n_lat_t, ctx_out, alpha):
    ttot, b, d = xall.shape
    w_a, w_b, w_ga, w_gb = w_parts
    ds = w_b.shape[1]
    tb = TB_MIX
    nblk = ttot // tb
    nl = n_lat_t // tb
    nc = nblk - nl
    ns2 = lp["bcat"].shape[-1]
    n_half = lp["bcat"].shape[1]

    scratch = [pltpu.VMEM((tb + lp["conv_w"].shape[1] - 1, b, d), F32),
               pltpu.VMEM((tb, b, d), F32),
               pltpu.VMEM((tb, b, d), F32),
               pltpu.VMEM((b, d), F32),
               pltpu.VMEM((2, tb, b, 2 * min(S5_CHUNK, ns2 // 2)), BF16),
               pltpu.VMEM((n_half, b, ns2), F32),
               pltpu.VMEM((tb * b, ds), F32)]

    def dir_args(dr):
        return (lp["conv_w"][dr], lp["conv_b"][dr], lp["wg"][dr], lp["ba"][dr], lp["bx"][dr], lp["sp"][dr],
                lp["bcat"][dr], lp["are"][dr], lp["aim"][dr], lp["ccat"][dr])

    def dir_specs(args):
        return [_const_spec(a.shape) for a in args]

    rev = lambda i: (nblk - 1 - i, 0, 0)
    tok_rev = lambda c: pl.BlockSpec((tb, b, c), rev)
    mod_rev = pl.BlockSpec((1, b, d), lambda i: (jnp.where(nblk - 1 - i >= nl, 1, 0), 0, 0))
    tok_shape = lambda c, dt: jax.ShapeDtypeStruct((ttot, b, c), dt)
    args = (w_a, w_b, w_ga, w_gb) + dir_args(1)
    ybl, ybs, xa, xb, ga, gb = pl.pallas_call(
        functools.partial(_mixer_rev_kernel, nc=nc),
        grid=(nblk,),
        in_specs=[tok_rev(d), mod_rev, mod_rev] + dir_specs(args),
        out_specs=[tok_rev(d), tok_rev(ds), tok_rev(d), tok_rev(ds), tok_rev(d), tok_rev(d)],
        out_shape=[tok_shape(d, BF16), tok_shape(ds, BF16), tok_shape(d, F32), tok_shape(ds, BF16),
                   tok_shape(d, BF16), tok_shape(d, BF16)],
        scratch_shapes=scratch,
        compiler_params=_cparams(("arbitrary",)),
        name="mixer_rev",
    )(xall, sc1, sh1, *args)

    fwd = lambda i: (jnp.where(i < nc, nl + i, i - nc), 0, 0)
    tok_fwd = lambda c: pl.BlockSpec((tb, b, c), fwd)
    if ctx_out:
        out_spec, t_out = tok_fwd(d), ttot
    else:
        out_spec, t_out = pl.BlockSpec((tb, b, d), lambda i: (jnp.maximum(i - nc, 0), 0, 0)), n_lat_t
    g1_spec = pl.BlockSpec((1, b, d), lambda i: (jnp.where(i < nc, 1, 0), 0, 0))
    args = dir_args(0)
    tail = (lp["d_skip"], lp["glu_w"], lp["glu_b"], lp["p_a"], lp["p_b"], lp["w_out"], lp["ln1_g"], lp["ln1_b"])
    return pl.pallas_call(
        functools.partial(_mixer_fwd_kernel, nc=nc, ctx_out=ctx_out, alpha=alpha),
        grid=(nblk,),
        in_specs=([tok_fwd(d), tok_fwd(ds)] + dir_specs(args)
                  + [tok_fwd(d), tok_fwd(ds), tok_fwd(d), tok_fwd(d), tok_fwd(d), g1_spec] + dir_specs(tail)),
        out_specs=out_spec,
        out_shape=jax.ShapeDtypeStruct((t_out, b, d), F32),
        scratch_shapes=scratch,
        compiler_params=_cparams(("arbitrary",)),
        name="mixer_fwd",
    )(xa, xb, *args, ybl, ybs, ga, gb, xall, g1, *tail)


def _ffn_kernel(*refs, piece, vertical, n_row_blocks, nk, alpha):
    if vertical:
        (x_ref, xt_ref, xbm_ref, sc_ref, sh_ref, g2_ref, wg_ref, wv_ref, wd_ref, cw_ref, cb_ref, lng_ref, lnb_ref,
         o_ref, u_s, gs_s) = refs[-16:]
    else:
        (x_ref, sc_ref, sh_ref, g2_ref, wg_ref, wv_ref, wd_ref, cw_ref, cb_ref, lng_ref, lnb_ref,
         o_ref, u_s, gs_s) = refs[-14:]
    j, k = pl.program_id(1), pl.program_id(2)
    nt, bb, d = x_ref.shape
    ck = wg_ref.shape[1]
    n_pre = 1 if vertical else 0
    n_main = nt // piece
    n_ext = n_main + 2 * n_pre
    vo = n_pre * piece
    stride = piece + 1 if vertical else piece

    mod = lambda v: (v * (1.0 + sc_ref[0]) + sh_ref[0]).astype(BF16)
    drs = (-1, 0, 1) if vertical else (0,)

    def fill_u(e):
        if vertical and e == 0:
            src = xt_ref[...]
        elif vertical and e == n_ext - 1:
            src = xbm_ref[...]
        else:
            src = x_ref[(e - n_pre) * piece:(e - n_pre + 1) * piece]
        u_s[e * piece:(e + 1) * piece] = mod(src)

    def gate_piece(e):
        g = jnp.dot(u_s[e * piece:(e + 1) * piece].reshape(piece * bb, d), wg_ref[...],
                    preferred_element_type=F32)
        if vertical and e == 0:
            g = jnp.where(j == 0, 0.0, g)
        if vertical and e == n_ext - 1:
            g = jnp.where(j == n_row_blocks - 1, 0.0, g)
        gs_s[1 + e * stride:1 + e * stride + piece] = g.reshape(piece, bb, ck)

    def body(first, last):
        for p in (range(n_ext + 1) if vertical else (0,)):
            gs_s[p * stride] = jnp.zeros((bb, ck), F32)
        if not vertical:
            gs_s[1 + nt] = jnp.zeros((bb, ck), F32)
        gates_done = 0
        for r in range(n_main):
            need = min(r + n_pre + 1, n_ext - 1)
            while gates_done <= need:
                if first:
                    fill_u(gates_done)
                gate_piece(gates_done)
                gates_done += 1
            rows_r = slice(r * piece, (r + 1) * piece)
            val = jnp.dot(u_s[vo + r * piece:vo + (r + 1) * piece].reshape(piece * bb, d), wv_ref[...],
                          preferred_element_type=F32)
            conv = None
            for dr in drs:
                for dc in (-1, 0, 1):
                    s0 = 1 + (r + n_pre + dr) * stride + dc
                    tap = (dr + 1) * 3 + dc + 1
                    term = cw_ref[tap:tap + 1] * gs_s[s0:s0 + piece]
                    conv = term if conv is None else conv + term
            act = _gelu_tanh(conv.reshape(piece * bb, ck) + cb_ref[...]) * val
            part = jnp.dot(act.astype(BF16), wd_ref[...], preferred_element_type=F32).reshape(piece, bb, d)
            acc = part if first else o_ref[rows_r] + part
            if last:
                v = alpha * x_ref[rows_r] + (1.0 + g2_ref[0]) * acc
                acc = _layer_norm(v, lng_ref[...], lnb_ref[...])
            o_ref[rows_r] = acc

    if nk == 1:
        body(True, True)
    else:
        pl.when(k == 0)(functools.partial(body, True, False))
        if nk > 2:
            pl.when(jnp.logical_and(k > 0, k < nk - 1))(functools.partial(body, False, False))
        pl.when(k == nk - 1)(functools.partial(body, False, True))


def _conv_ffn(xall, sc, sh, g2, lp, n_lat_t, ctx_out, alpha):
    ttot, b, d = xall.shape
    dff = lp["ffn_down"].shape[0]
    ck = min(FFN_CK, dff)
    nk = dff // ck
    bb = min(FFN_BB, b)
    tc = ttot - n_lat_t
    w_up, w_down = lp["ffn_up"], lp["ffn_down"]

    wspecs = [pl.BlockSpec((d, ck), lambda ib, j, k: (0, k)),
              pl.BlockSpec((d, ck), lambda ib, j, k: (0, nk + k)),
              pl.BlockSpec((ck, d), lambda ib, j, k: (k, 0)),
              pl.BlockSpec((9, 1, ck), lambda ib, j, k: (0, 0, k)),
              pl.BlockSpec((1, ck), lambda ib, j, k: (0, k)),
              pl.BlockSpec((1, d), lambda ib, j, k: (0, 0)),
              pl.BlockSpec((1, d), lambda ib, j, k: (0, 0))]
    wargs = (w_up, w_up, w_down, lp["ffn_conv_w"], lp["ffn_conv_b"], lp["ln2_g"], lp["ln2_b"])

    w = GRID_W
    n_img_rows = n_lat_t // w
    r = min(FFN_ROWS, n_img_rows)
    nt = r * w
    nrb = n_img_rows // r
    mod_lat = pl.BlockSpec((1, bb, d), lambda ib, j, k: (0, ib, 0))
    t_out = ttot if ctx_out else n_lat_t
    out = pl.pallas_call(
        functools.partial(_ffn_kernel, piece=w, vertical=True, n_row_blocks=nrb, nk=nk, alpha=alpha),
        grid=(b // bb, nrb, nk),
        in_specs=[pl.BlockSpec((nt, bb, d), lambda ib, j, k: (j, ib, 0)),
                  pl.BlockSpec((w, bb, d), lambda ib, j, k: (jnp.maximum(j * r - 1, 0), ib, 0),
                               pipeline_mode=pl.Buffered(1)),
                  pl.BlockSpec((w, bb, d), lambda ib, j, k: (jnp.minimum(j * r + r, n_img_rows - 1), ib, 0),
                               pipeline_mode=pl.Buffered(1)),
                  mod_lat, mod_lat, mod_lat] + wspecs,
        out_specs=pl.BlockSpec((nt, bb, d), lambda ib, j, k: (j, ib, 0)),
        out_shape=jax.ShapeDtypeStruct((t_out, b, d), F32),
        scratch_shapes=[pltpu.VMEM((nt + 2 * w, bb, d), BF16),
                        pltpu.VMEM(((r + 2) * (w + 1) + 1, bb, ck), F32)],
        compiler_params=_cparams(("arbitrary", "arbitrary", "arbitrary")),
        name="conv_ffn_latent",
    )(xall, xall, xall, sc, sh, g2, *wargs)
    if not ctx_out:
        return out

    cblk = n_lat_t // tc
    mod_ctx = pl.BlockSpec((1, bb, d), lambda ib, j, k: (1, ib, 0))
    return pl.pallas_call(
        functools.partial(_ffn_kernel, piece=min(w, tc), vertical=False, n_row_blocks=1, nk=nk, alpha=alpha),
        grid=(b // bb, 1, nk),
        in_specs=[pl.BlockSpec(memory_space=pl.ANY),
                  pl.BlockSpec((tc, bb, d), lambda ib, j, k: (cblk, ib, 0), pipeline_mode=pl.Buffered(1)),
                  mod_ctx, mod_ctx, mod_ctx] + wspecs,
        out_specs=pl.BlockSpec((tc, bb, d), lambda ib, j, k: (cblk, ib, 0)),
        out_shape=jax.ShapeDtypeStruct((ttot, b, d), F32),
        scratch_shapes=[pltpu.VMEM((tc, bb, d), BF16),
                        pltpu.VMEM((tc + 2, bb, ck), F32)],
        input_output_aliases={0: 0},
        compiler_params=_cparams(("arbitrary", "arbitrary", "arbitrary")),
        name="conv_ffn_context",
    )(out, xall, sc, sh, g2, *wargs)


def _pack_gate_weights(wa, wx):
    h, hd = wa.shape[-3], wa.shape[-1]
    d = h * hd
    tile = min(MXU_TILE, d)
    hpt = tile // hd
    lead = wa.shape[:-3]
    eye = jnp.eye(hpt, dtype=wa.dtype)

    def dense(w):
        w = w.reshape(lead + (d // tile, hpt, hd, hd))
        return jnp.einsum('...qhij,hg->...qhigj', w, eye).reshape(lead + (d // tile, tile, tile))

    return jnp.concatenate([dense(wa), dense(wx)], axis=-1).astype(BF16)


def _pack_s5(ar, ai, bbr, bbi, c_re, c_im):
    lead = bbr.shape[:-3]
    g, p, s = bbr.shape[-3:]
    gpt = min(MXU_TILE // s, g)
    nh = g // gpt
    eye = jnp.eye(gpt, dtype=F32)

    def b_dense(bb_):
        bb_ = bb_.reshape(lead + (nh, gpt, p, s))
        return jnp.einsum('...hgpc,gk->...hgckp', bb_, eye).reshape(lead + (nh, gpt * s, gpt * p))

    def c_dense(cc):
        cc = cc.reshape(lead + (nh, gpt, s, p))
        return jnp.einsum('...hgcp,gk->...hgpkc', cc, eye).reshape(lead + (nh, gpt * p, gpt * s))

    ns = gpt * p
    cw = min(S5_CHUNK, ns)
    nchunk = ns // cw
    b_r = b_dense(bbr).reshape(lead + (nh, gpt * s, nchunk, 1, cw))
    b_i = b_dense(bbi).reshape(lead + (nh, gpt * s, nchunk, 1, cw))
    bcat = jnp.concatenate([b_r, b_i], axis=-2).reshape(lead + (nh, gpt * s, 2 * ns)).astype(BF16)
    c_r = c_dense(c_re).reshape(lead + (nh, nchunk, 1, cw, gpt * s))
    c_i = c_dense(-c_im).reshape(lead + (nh, nchunk, 1, cw, gpt * s))
    ccat = jnp.concatenate([c_r, c_i], axis=-3).reshape(lead + (nh, 2 * ns, gpt * s)).astype(BF16)
    are = ar.reshape(lead + (nh, 1, gpt * p))
    aim = ai.reshape(lead + (nh, 1, gpt * p))
    return bcat, are, aim, ccat


def kernel(x, c, ctx, c_ctx, ada_w, ada_b, w_in, lru_conv_w, lru_conv_b, lru_wa, lru_ba, lru_wx, lru_bx, lru_lam, s5_lam_re, s5_lam_im, s5_log_dt, s5_b_re, s5_b_im, s5_c_re, s5_c_im, s5_d, s5_glu_w, s5_glu_b, p_a, p_b, w_out, ln1_g, ln1_b, ffn_up, ffn_conv_w, ffn_conv_b, ffn_down, ln2_g, ln2_b):
    bsz, seq, d = x.shape
    n_ctx = ctx.shape[1]
    depth = ada_w.shape[0]
    d_lru = lru_lam.shape[-1]
    d_s5 = s5_d.shape[-1]
    dff = ffn_down.shape[1]
    alpha = (2.0 * depth) ** 0.25
    assert seq % n_ctx == 0 and seq % GRID_W == 0 and n_ctx % TB_MIX == 0

    m_rows = -(-(bsz + 1) // 8) * 8
    cond = jnp.zeros((m_rows, d), F32).at[:bsz].set(c).at[bsz].set(c_ctx)
    modv = _modulation(cond, ada_w, ada_b).reshape(depth, m_rows, 6, d)

    ar, ai, bbr, bbi = _s5_discretise(s5_lam_re, s5_lam_im, s5_log_dt, s5_b_re, s5_b_im)
    bcat, are, aim, ccat = _pack_s5(ar, ai, bbr, bbi, s5_c_re, s5_c_im)
    sp = _softplus_neg(lru_lam.reshape(depth * 2, d_lru)).reshape(depth, 2, 1, d_lru)
    wg = _pack_gate_weights(lru_wa, lru_wx)
    w_in16 = w_in.astype(BF16)
    row = lambda a: a[..., None, :]

    xall = jnp.concatenate([jnp.transpose(x, (1, 0, 2)), jnp.transpose(ctx, (1, 0, 2))], axis=0)
    for l in range(depth):
        ctx_out = l < depth - 1
        lat = modv[l, :bsz]
        cx = jnp.broadcast_to(modv[l, bsz][None], (bsz, 6, d))
        mods = jnp.stack([lat, cx], axis=0)
        sh1, sc1, g1, sh2, sc2, g2 = (mods[:, :, n] for n in range(6))
        lp = dict(conv_w=lru_conv_w[l][:, :, None, :], conv_b=row(lru_conv_b[l]), wg=wg[l], ba=row(lru_ba[l]),
                  bx=row(lru_bx[l]), sp=sp[l], bcat=bcat[l], are=are[l], aim=aim[l], ccat=ccat[l],
                  d_skip=row(s5_d[l]), glu_w=s5_glu_w[l].astype(BF16), glu_b=row(s5_glu_b[l]),
                  p_a=p_a[l].astype(BF16), p_b=p_b[l].astype(BF16), w_out=w_out[l].astype(BF16),
                  ln1_g=row(ln1_g[l]), ln1_b=row(ln1_b[l]),
                  ffn_up=ffn_up[l].astype(BF16), ffn_down=ffn_down[l].astype(BF16),
                  ffn_conv_w=ffn_conv_w[l].reshape(9, 1, dff), ffn_conv_b=row(ffn_conv_b[l]),
                  ln2_g=row(ln2_g[l]), ln2_b=row(ln2_b[l]))
        wl = w_in16[l]
        w_parts = (wl[:, :d_lru], wl[:, d_lru:d_lru + d_s5], wl[:, d_lru + d_s5:d_lru + d_s5 + d],
                   wl[:, d_lru + d_s5 + d:])
        xmid = _mixer(xall, sc1, sh1, g1, w_parts, lp, seq, ctx_out, alpha)
        xall = _conv_ffn(xmid, sc2, sh2, g2, lp, seq, ctx_out, alpha)
    return jnp.transpose(xall[:seq], (1, 0, 2))
```

```python
import functools
import math

import jax
import jax.numpy as jnp
from jax import lax
from jax.experimental import pallas as pl
from jax.experimental.pallas import tpu as pltpu

F32 = jnp.float32
BF16 = jnp.bfloat16

GRID_W = 64
LRU_C = 8.0
LN_EPS = 1e-5
S5_GROUP = 16
MXU_TILE = 256
VMEM_LIMIT = 60 * 1024 * 1024

TB_LAYOUT = 32
TB_MIX = 16
FFN_ROWS = 4
FFN_BB = 8
FFN_CK = 512
LRU_CHUNK = 512
S5_CHUNK = 256


def _cparams(sem):
    return pltpu.CompilerParams(dimension_semantics=sem, vmem_limit_bytes=VMEM_LIMIT)


def _const_spec(shape):
    nd = len(shape)
    return pl.BlockSpec(shape, lambda *_: (0,) * nd, pipeline_mode=pl.Buffered(1))


def _gelu_tanh(x):
    return 0.5 * x * (1.0 + jnp.tanh(math.sqrt(2.0 / math.pi) * (x + 0.044715 * (x * x * x))))


def _sigmoid(x):
    return 1.0 / (1.0 + jnp.exp(-x))


def _layer_norm(v, g, b):
    mu = jnp.mean(v, axis=-1, keepdims=True)
    xc = v - mu
    var = jnp.mean(xc * xc, axis=-1, keepdims=True)
    return xc * lax.rsqrt(var + LN_EPS) * g + b


def _to_time_major_kernel(x_ref, c_ref, o_ref, *, nl):
    i = pl.program_id(0)

    @pl.when(i < nl)
    def _():
        o_ref[...] = pltpu.einshape("btd->tbd", x_ref[...])

    @pl.when(i >= nl)
    def _():
        o_ref[...] = pltpu.einshape("btd->tbd", c_ref[...])


def _to_time_major(x, ctx):
    b, t, d = x.shape
    tc = ctx.shape[1]
    tb = TB_LAYOUT
    nl = t // tb
    return pl.pallas_call(
        functools.partial(_to_time_major_kernel, nl=nl),
        grid=((t + tc) // tb,),
        in_specs=[pl.BlockSpec((b, tb, d), lambda i: (0, jnp.minimum(i, nl - 1), 0)),
                  pl.BlockSpec((b, tb, d), lambda i: (0, jnp.maximum(i - nl, 0), 0))],
        out_specs=pl.BlockSpec((tb, b, d), lambda i: (i, 0, 0)),
        out_shape=jax.ShapeDtypeStruct((t + tc, b, d), x.dtype),
        compiler_params=_cparams(("arbitrary",)),
        name="to_time_major",
    )(x, ctx)


def _to_batch_major_kernel(x_ref, o_ref):
    o_ref[...] = pltpu.einshape("tbd->btd", x_ref[...])


def _to_batch_major(xt):
    t, b, d = xt.shape
    tb = TB_LAYOUT
    return pl.pallas_call(
        _to_batch_major_kernel,
        grid=(t // tb,),
        in_specs=[pl.BlockSpec((tb, b, d), lambda i: (i, 0, 0))],
        out_specs=pl.BlockSpec((b, tb, d), lambda i: (0, i, 0)),
        out_shape=jax.ShapeDtypeStruct((b, t, d), xt.dtype),
        compiler_params=_cparams(("arbitrary",)),
        name="to_batch_major",
    )(xt)


def _mod_kernel(c_ref, w_ref, b_ref, o_ref):
    c = c_ref[...]
    s = (c * _sigmoid(c)).astype(BF16)
    o_ref[0] = jnp.dot(s, w_ref[0].astype(BF16), preferred_element_type=F32) + b_ref[0]


def _modulation(cond, ada_w, ada_b):
    depth, d, n = ada_w.shape
    m = cond.shape[0]
    tn = min(n, 1536)
    return pl.pallas_call(
        _mod_kernel,
        grid=(depth, n // tn),
        in_specs=[pl.BlockSpec((m, d), lambda l, j: (0, 0)),
                  pl.BlockSpec((1, d, tn), lambda l, j: (l, 0, j)),
                  pl.BlockSpec((1, 1, tn), lambda l, j: (l, 0, j))],
        out_specs=pl.BlockSpec((1, m, tn), lambda l, j: (l, 0, j)),
        out_shape=jax.ShapeDtypeStruct((depth, m, n), F32),
        compiler_params=_cparams(("arbitrary", "arbitrary")),
        name="adaln_mod",
    )(cond, ada_w, ada_b.reshape(depth, 1, n))


def _s5_disc_kernel(lr_ref, li_ref, ldt_ref, br_ref, bi_ref, ar_ref, ai_ref, bbr_ref, bbi_ref):
    lr, li = lr_ref[...], li_ref[...]
    dt = jnp.exp(ldt_ref[...])
    mag = jnp.exp(lr * dt)
    ar = mag * jnp.cos(li * dt)
    ai = mag * jnp.sin(li * dt)
    den = lr * lr + li * li
    fr = ((ar - 1.0) * lr + ai * li) / den
    fi = (ai * lr - (ar - 1.0) * li) / den
    br, bi = br_ref[...], bi_ref[...]
    ar_ref[...] = ar
    ai_ref[...] = ai
    bbr_ref[...] = fr * br - fi * bi
    bbi_ref[...] = fr * bi + fi * br


def _s5_discretise(lam_re, lam_im, log_dt, b_re, b_im):
    lead = b_re.shape[:-2]
    p, s = b_re.shape[-2:]
    n = math.prod(lead)
    full = lead + (p, s)
    flat = lambda a: jnp.broadcast_to(a, full).reshape(n, p * s)
    args = (flat(lam_re[..., None]), flat(lam_im[..., None]), flat(log_dt[..., None, None]), flat(b_re), flat(b_im))
    rb = 8 if n % 8 == 0 else n
    spec = pl.BlockSpec((rb, p * s), lambda i: (i, 0))
    outs = pl.pallas_call(
        _s5_disc_kernel,
        grid=(n // rb,),
        in_specs=[spec] * 5,
        out_specs=[spec] * 4,
        out_shape=[jax.ShapeDtypeStruct((n, p * s), F32)] * 4,
        compiler_params=_cparams(("arbitrary",)),
        name="s5_discretise",
    )(*args)
    ar, ai, bbr, bbi = (o.reshape(full) for o in outs)
    return ar[..., 0], ai[..., 0], bbr, bbi


def _softplus_neg_kernel(lam_ref, o_ref):
    z = -lam_ref[...]
    o_ref[...] = LRU_C * (jnp.maximum(z, 0.0) + jnp.log(1.0 + jnp.exp(-jnp.abs(z))))


def _softplus_neg(lam2d):
    return pl.pallas_call(
        _softplus_neg_kernel,
        out_shape=jax.ShapeDtypeStruct(lam2d.shape, F32),
        name="lru_softplus",
    )(lam2d)


def _mixer_direction(reverse, seq_start, any_start, xa_ref, xb_ref, cw_ref, cb_ref, wg_ref, ba_ref, bx_ref,
                     sp_ref, bcat_ref, are_ref, aim_ref, ccat_ref, ext_s, a_s, g_s, hl_s, hb_s, hs_s, ys_s,
                     inproj=None):
    tb, bb, d = xa_ref.shape
    rows = tb * bb
    k_w = cw_ref.shape[0]
    halo = k_w - 1
    ds = xb_ref.shape[2]
    n_half = bcat_ref.shape[0]
    hw = ds // n_half
    ns = are_ref.shape[2]
    cw = hb_s.shape[3] // 2
    nt = wg_ref.shape[0]
    tw = d // nt
    x_lo = 0 if reverse else halo
    h_new, h_old = (tb, 0) if reverse else (0, tb)
    order = lambda s: tb - 1 - s if reverse else s

    @pl.when(any_start)
    def _():
        hl_s[...] = jnp.zeros_like(hl_s)
        hs_s[...] = jnp.zeros_like(hs_s)

    @pl.when(seq_start)
    def _():
        ext_s[h_new:h_new + halo] = jnp.zeros((halo, bb, d), F32)

    @pl.when(jnp.logical_not(seq_start))
    def _():
        ext_s[h_new:h_new + halo] = ext_s[h_old:h_old + halo]

    fillers = []
    u_half = {}
    if inproj is None:
        ext_s[x_lo:x_lo + tb] = xa_ref[...]
        u = xb_ref[...].reshape(rows, ds)
        for hf in range(n_half):
            u_half[hf] = u[:, hf * hw:(hf + 1) * hw]
    else:
        x_ref, sc_ref, sh_ref, wa_ref, wb_ref, wga_ref, wgb_ref, ga_ref, gb_ref = inproj
        xm = (x_ref[...] * (1.0 + sc_ref[0]) + sh_ref[0]).reshape(rows, d).astype(BF16)

        def project_xb(hf):
            v = jnp.dot(xm, wb_ref[:, hf * hw:(hf + 1) * hw], preferred_element_type=F32).astype(BF16)
            xb_ref[:, :, hf * hw:(hf + 1) * hw] = v.reshape(tb, bb, hw)
            u_half[hf] = v

        def project_xa(q):
            sl = slice(q * tw, (q + 1) * tw)
            v = jnp.dot(xm, wa_ref[:, sl], preferred_element_type=F32).reshape(tb, bb, tw)
            xa_ref[:, :, sl] = v
            ext_s[x_lo:x_lo + tb, :, sl] = v

        def project_gate(w_ref, o_ref, q):
            sl = slice(q * tw, (q + 1) * tw)
            v = jnp.dot(xm, w_ref[:, sl], preferred_element_type=F32)
            o_ref[:, :, sl] = v.reshape(tb, bb, tw).astype(o_ref.dtype)

        project_xb(0)
        project_xa(0)
        fillers = ([functools.partial(project_xb, hf) for hf in range(1, n_half)]
                   + [functools.partial(project_xa, q) for q in range(1, nt)]
                   + [functools.partial(project_gate, w, o, q)
                      for w, o in ((wga_ref, ga_ref), (wgb_ref, gb_ref)) for q in range(nt)])

    def s5_project(hf, c):
        cols = slice(c * 2 * cw, (c + 1) * 2 * cw)
        bu = jnp.dot(u_half[hf], bcat_ref[hf, :, cols], preferred_element_type=F32)
        return bu.reshape(tb, bb, 2 * cw)

    def s5_scan(hf, c, bu, slot):
        lo = c * 2 * cw
        ar = jnp.broadcast_to(are_ref[hf, :, c * cw:(c + 1) * cw], (bb, cw))
        ai = jnp.broadcast_to(aim_ref[hf, :, c * cw:(c + 1) * cw], (bb, cw))
        hr, hi = hs_s[hf, :, lo:lo + cw], hs_s[hf, :, lo + cw:lo + 2 * cw]
        for s in range(tb):
            t = order(s)
            hr, hi = ar * hr - ai * hi + bu[t, :, :cw], ar * hi + ai * hr + bu[t, :, cw:]
            hb_s[slot, t, :, :cw] = hr.astype(BF16)
            hb_s[slot, t, :, cw:] = hi.astype(BF16)
        hs_s[hf, :, lo:lo + cw] = hr
        hs_s[hf, :, lo + cw:lo + 2 * cw] = hi

    def s5_readout(hf, c, slot):
        cols = slice(c * 2 * cw, (c + 1) * 2 * cw)
        part = jnp.dot(hb_s[slot].reshape(rows, 2 * cw), ccat_ref[hf, cols, :], preferred_element_type=F32)
        if c == 0:
            ys_s[:, hf * hw:(hf + 1) * hw] = part
        else:
            ys_s[:, hf * hw:(hf + 1) * hw] += part

    def gate_matmul(q):
        sl = slice(q * tw, (q + 1) * tw)
        xq = cb_ref[:, sl] + cw_ref[k_w - 1:k_w, :, sl] * ext_s[x_lo:x_lo + tb, :, sl]
        for k in range(k_w - 1):
            off = halo - k if reverse else k
            xq = xq + cw_ref[k:k + 1, :, sl] * ext_s[off:off + tb, :, sl]
        xq = xq.reshape(rows, tw)
        return q, xq, jnp.dot(xq.astype(BF16), wg_ref[q], preferred_element_type=F32)

    def gate_coefficients(q, xq, z):
        sl = slice(q * tw, (q + 1) * tw)
        r = _sigmoid(z[:, :tw] + ba_ref[:, sl])
        gi = _sigmoid(z[:, tw:] + bx_ref[:, sl])
        nla = r * sp_ref[:, sl]
        a = jnp.exp(-nla)
        gx = jnp.sqrt(jnp.tanh(nla) * (a * a + 1.0)) * (gi * xq)
        a_s[:, :, sl] = a.reshape(tb, bb, tw)
        g_s[:, :, sl] = gx.reshape(tb, bb, tw)

    pieces = [(hf, c) for hf in range(n_half) for c in range(ns // cw)]
    tiles_started = 0
    bu_next = s5_project(*pieces[0])
    pending = None
    per_piece = -(-len(fillers) // max(1, len(pieces) - 1))
    for n, (hf, c) in enumerate(pieces):
        bu = bu_next
        for _ in range(min(per_piece, len(fillers))):
            fillers.pop(0)()
        if n + 1 < len(pieces):
            bu_next = s5_project(*pieces[n + 1])
        started = None
        if tiles_started < nt and n * nt >= tiles_started * len(pieces):
            started = gate_matmul(tiles_started)
            tiles_started += 1
        s5_scan(hf, c, bu, n % 2)
        s5_readout(hf, c, n % 2)
        if pending is not None:
            gate_coefficients(*pending)
        pending = started
    if pending is not None:
        gate_coefficients(*pending)
    while fillers:
        fillers.pop(0)()
    for q in range(tiles_started, nt):
        gate_coefficients(*gate_matmul(q))

    cl = min(LRU_CHUNK, d)
    for c0 in range(0, d, cl):
        cs = slice(c0, c0 + cl)
        h = hl_s[:, cs]
        for s in range(tb):
            t = order(s)
            h = a_s[t, :, cs] * h + g_s[t, :, cs]
            g_s[t, :, cs] = h
        hl_s[:, cs] = h


def _mixer_rev_kernel(x_ref, sc_ref, sh_ref, wa_ref, wb_ref, wga_ref, wgb_ref, cw_ref, cb_ref, wg_ref, ba_ref,
                      bx_ref, sp_ref, bcat_ref, are_ref, aim_ref, ccat_ref, ybl_ref, ybs_ref, xa_ref, xb_ref, ga_ref,
                      gb_ref, ext_s, a_s, g_s, hl_s, hb_s, hs_s, ys_s, *, nc):
    i = pl.program_id(0)
    tb, bb, _ = xa_ref.shape
    _mixer_direction(True, jnp.logical_or(i == 0, i == nc), i == 0, xa_ref, xb_ref, cw_ref, cb_ref, wg_ref,
                     ba_ref, bx_ref, sp_ref, bcat_ref, are_ref, aim_ref, ccat_ref, ext_s, a_s, g_s, hl_s, hb_s,
                     hs_s, ys_s,
                     inproj=(x_ref, sc_ref, sh_ref, wa_ref, wb_ref, wga_ref, wgb_ref, ga_ref, gb_ref))
    ybl_ref[...] = g_s[...].astype(ybl_ref.dtype)
    ybs_ref[...] = ys_s[...].reshape(tb, bb, ys_s.shape[-1]).astype(ybs_ref.dtype)


def _mixer_fwd_kernel(xa_ref, xb_ref, cw_ref, cb_ref, wg_ref, ba_ref, bx_ref, sp_ref, bcat_ref, are_ref,
                      aim_ref, ccat_ref, ybl_ref, ybs_ref, ga_ref, gb_ref, x_ref, g1_ref, dsk_ref, gluw_ref,
                      glub_ref, pa_ref, pb_ref, wo_ref, lng_ref, lnb_ref, o_ref,
                      ext_s, a_s, g_s, hl_s, hb_s, hs_s, ys_s, *, nc, ctx_out, alpha):
    i = pl.program_id(0)
    tb, bb, d = xa_ref.shape
    rows = tb * bb
    _mixer_direction(False, jnp.logical_or(i == 0, i == nc), i == 0, xa_ref, xb_ref, cw_ref, cb_ref, wg_ref,
                     ba_ref, bx_ref, sp_ref, bcat_ref, are_ref, aim_ref, ccat_ref, ext_s, a_s, g_s, hl_s, hb_s,
                     hs_s, ys_s)

    def merge():
        ds = xb_ref.shape[2]
        ya = (g_s[...] + ybl_ref[...].astype(F32)).reshape(rows, d).astype(BF16)
        ta = jnp.dot(ya, pa_ref[...], preferred_element_type=F32)
        u = xb_ref[...].reshape(rows, ds).astype(F32)
        yb = ys_s[...] + ybs_ref[...].reshape(rows, ds).astype(F32) + dsk_ref[...] * u
        z = _gelu_tanh(yb)
        zg = jnp.dot(z.astype(BF16), gluw_ref[...], preferred_element_type=F32) + glub_ref[...]
        ma = _sigmoid(ga_ref[...].reshape(rows, d).astype(F32)) * ta
        yb = (z * _sigmoid(zg)).astype(BF16)
        tbp = jnp.dot(yb, pb_ref[...], preferred_element_type=F32)
        m = ma + _sigmoid(gb_ref[...].reshape(rows, d).astype(F32)) * tbp
        out = jnp.dot(m.astype(BF16), wo_ref[...], preferred_element_type=F32).reshape(tb, bb, d)
        v = alpha * x_ref[...] + (1.0 + g1_ref[0]) * out
        o_ref[...] = _layer_norm(v, lng_ref[...], lnb_ref[...])

    if ctx_out:
        merge()
    else:
        pl.when(i >= nc)(merge)


def _mixer(xall, sc1, sh1, g1, w_parts, lp, n_lat_t, ctx_out, alpha):
    ttot, b, d = xall.shape
    w_a, w_b, w_ga, w_gb = w_parts
    ds = w_b.shape[1]
    tb = TB_MIX
    nblk = ttot // tb
    nl = n_lat_t // tb
    nc = nblk - nl
    ns2 = lp["bcat"].shape[-1]
    n_half = lp["bcat"].shape[1]

    scratch = [pltpu.VMEM((tb + lp["conv_w"].shape[1] - 1, b, d), F32),
               pltpu.VMEM((tb, b, d), F32),
               pltpu.VMEM((tb, b, d), F32),
               pltpu.VMEM((b, d), F32),
               pltpu.VMEM((2, tb, b, 2 * min(S5_CHUNK, ns2 // 2)), BF16),
               pltpu.VMEM((n_half, b, ns2), F32),
               pltpu.VMEM((tb * b, ds), F32)]

    def dir_args(dr):
        return (lp["conv_w"][dr], lp["conv_b"][dr], lp["wg"][dr], lp["ba"][dr], lp["bx"][dr], lp["sp"][dr],
                lp["bcat"][dr], lp["are"][dr], lp["aim"][dr], lp["ccat"][dr])

    def dir_specs(args):
        return [_const_spec(a.shape) for a in args]

    rev = lambda i: (nblk - 1 - i, 0, 0)
    tok_rev = lambda c: pl.BlockSpec((tb, b, c), rev)
    mod_rev = pl.BlockSpec((1, b, d), lambda i: (jnp.where(nblk - 1 - i >= nl, 1, 0), 0, 0))
    tok_shape = lambda c, dt: jax.ShapeDtypeStruct((ttot, b, c), dt)
    args = (w_a, w_b, w_ga, w_gb) + dir_args(1)
    ybl, ybs, xa, xb, ga, gb = pl.pallas_call(
        functools.partial(_mixer_rev_kernel, nc=nc),
        grid=(nblk,),
        in_specs=[tok_rev(d), mod_rev, mod_rev] + dir_specs(args),
        out_specs=[tok_rev(d), tok_rev(ds), tok_rev(d), tok_rev(ds), tok_rev(d), tok_rev(d)],
        out_shape=[tok_shape(d, BF16), tok_shape(ds, BF16), tok_shape(d, F32), tok_shape(ds, BF16),
                   tok_shape(d, BF16), tok_shape(d, BF16)],
        scratch_shapes=scratch,
        compiler_params=_cparams(("arbitrary",)),
        name="mixer_rev",
    )(xall, sc1, sh1, *args)

    fwd = lambda i: (jnp.where(i < nc, nl + i, i - nc), 0, 0)
    tok_fwd = lambda c: pl.BlockSpec((tb, b, c), fwd)
    if ctx_out:
        out_spec, t_out = tok_fwd(d), ttot
    else:
        out_spec, t_out = pl.BlockSpec((tb, b, d), lambda i: (jnp.maximum(i - nc, 0), 0, 0)), n_lat_t
    g1_spec = pl.BlockSpec((1, b, d), lambda i: (jnp.where(i < nc, 1, 0), 0, 0))
    args = dir_args(0)
    tail = (lp["d_skip"], lp["glu_w"], lp["glu_b"], lp["p_a"], lp["p_b"], lp["w_out"], lp["ln1_g"], lp["ln1_b"])
    return pl.pallas_call(
        functools.partial(_mixer_fwd_kernel, nc=nc, ctx_out=ctx_out, alpha=alpha),
        grid=(nblk,),
        in_specs=([tok_fwd(d), tok_fwd(ds)] + dir_specs(args)
                  + [tok_fwd(d), tok_fwd(ds), tok_fwd(d), tok_fwd(d), tok_fwd(d), g1_spec] + dir_specs(tail)),
        out_specs=out_spec,
        out_shape=jax.ShapeDtypeStruct((t_out, b, d), F32),
        scratch_shapes=scratch,
        compiler_params=_cparams(("arbitrary",)),
        name="mixer_fwd",
    )(xa, xb, *args, ybl, ybs, ga, gb, xall, g1, *tail)


def _ffn_kernel(*refs, piece, vertical, n_row_blocks, nk, alpha):
    if vertical:
        (x_ref, xt_ref, xbm_ref, sc_ref, sh_ref, g2_ref, wg_ref, wv_ref, wd_ref, cw_ref, cb_ref, lng_ref, lnb_ref,
         o_ref, u_s, gs_s) = refs[-16:]
    else:
        (x_ref, sc_ref, sh_ref, g2_ref, wg_ref, wv_ref, wd_ref, cw_ref, cb_ref, lng_ref, lnb_ref,
         o_ref, u_s, gs_s) = refs[-14:]
    j, k = pl.program_id(1), pl.program_id(2)
    nt, bb, d = x_ref.shape
    ck = wg_ref.shape[1]
    n_pre = 1 if vertical else 0
    n_main = nt // piece
    n_ext = n_main + 2 * n_pre
    vo = n_pre * piece
    stride = piece + 1 if vertical else piece

    mod = lambda v: (v * (1.0 + sc_ref[0]) + sh_ref[0]).astype(BF16)
    drs = (-1, 0, 1) if vertical else (0,)

    def fill_u(e):
        if vertical and e == 0:
            src = xt_ref[...]
        elif vertical and e == n_ext - 1:
            src = xbm_ref[...]
        else:
            src = x_ref[(e - n_pre) * piece:(e - n_pre + 1) * piece]
        u_s[e * piece:(e + 1) * piece] = mod(src)

    def gate_piece(e):
        g = jnp.dot(u_s[e * piece:(e + 1) * piece].reshape(piece * bb, d), wg_ref[...],
                    preferred_element_type=F32)
        if vertical and e == 0:
            g = jnp.where(j == 0, 0.0, g)
        if vertical and e == n_ext - 1:
            g = jnp.where(j == n_row_blocks - 1, 0.0, g)
        gs_s[1 + e * stride:1 + e * stride + piece] = g.reshape(piece, bb, ck)

    def body(first, last):
        for p in (range(n_ext + 1) if vertical else (0,)):
            gs_s[p * stride] = jnp.zeros((bb, ck), F32)
        if not vertical:
            gs_s[1 + nt] = jnp.zeros((bb, ck), F32)
        gates_done = 0
        for r in range(n_main):
            need = min(r + n_pre + 1, n_ext - 1)
            while gates_done <= need:
                if first:
                    fill_u(gates_done)
                gate_piece(gates_done)
                gates_done += 1
            rows_r = slice(r * piece, (r + 1) * piece)
            val = jnp.dot(u_s[vo + r * piece:vo + (r + 1) * piece].reshape(piece * bb, d), wv_ref[...],
                          preferred_element_type=F32)
            conv = None
            for dr in drs:
                for dc in (-1, 0, 1):
                    s0 = 1 + (r + n_pre + dr) * stride + dc
                    tap = (dr + 1) * 3 + dc + 1
                    term = cw_ref[tap:tap + 1] * gs_s[s0:s0 + piece]
                    conv = term if conv is None else conv + term
            act = _gelu_tanh(conv.reshape(piece * bb, ck) + cb_ref[...]) * val
            part = jnp.dot(act.astype(BF16), wd_ref[...], preferred_element_type=F32).reshape(piece, bb, d)
            acc = part if first else o_ref[rows_r] + part
            if last:
                v = alpha * x_ref[rows_r] + (1.0 + g2_ref[0]) * acc
                acc = _layer_norm(v, lng_ref[...], lnb_ref[...])
            o_ref[rows_r] = acc

    if nk == 1:
        body(True, True)
    else:
        pl.when(k == 0)(functools.partial(body, True, False))
        if nk > 2:
            pl.when(jnp.logical_and(k > 0, k < nk - 1))(functools.partial(body, False, False))
        pl.when(k == nk - 1)(functools.partial(body, False, True))


def _conv_ffn(xall, sc, sh, g2, lp, n_lat_t, ctx_out, alpha):
    ttot, b, d = xall.shape
    dff = lp["ffn_down"].shape[0]
    ck = min(FFN_CK, dff)
    nk = dff // ck
    bb = min(FFN_BB, b)
    tc = ttot - n_lat_t
    w_up, w_down = lp["ffn_up"], lp["ffn_down"]

    wspecs = [pl.BlockSpec((d, ck), lambda ib, j, k: (0, k)),
              pl.BlockSpec((d, ck), lambda ib, j, k: (0, nk + k)),
              pl.BlockSpec((ck, d), lambda ib, j, k: (k, 0)),
              pl.BlockSpec((9, 1, ck), lambda ib, j, k: (0, 0, k)),
              pl.BlockSpec((1, ck), lambda ib, j, k: (0, k)),
              pl.BlockSpec((1, d), lambda ib, j, k: (0, 0)),
              pl.BlockSpec((1, d), lambda ib, j, k: (0, 0))]
    wargs = (w_up, w_up, w_down, lp["ffn_conv_w"], lp["ffn_conv_b"], lp["ln2_g"], lp["ln2_b"])

    w = GRID_W
    n_img_rows = n_lat_t // w
    r = min(FFN_ROWS, n_img_rows)
    nt = r * w
    nrb = n_img_rows // r
    mod_lat = pl.BlockSpec((1, bb, d), lambda ib, j, k: (0, ib, 0))
    t_out = ttot if ctx_out else n_lat_t
    out = pl.pallas_call(
        functools.partial(_ffn_kernel, piece=w, vertical=True, n_row_blocks=nrb, nk=nk, alpha=alpha),
        grid=(b // bb, nrb, nk),
        in_specs=[pl.BlockSpec((nt, bb, d), lambda ib, j, k: (j, ib, 0)),
                  pl.BlockSpec((w, bb, d), lambda ib, j, k: (jnp.maximum(j * r - 1, 0), ib, 0),
                               pipeline_mode=pl.Buffered(1)),
                  pl.BlockSpec((w, bb, d), lambda ib, j, k: (jnp.minimum(j * r + r, n_img_rows - 1), ib, 0),
                               pipeline_mode=pl.Buffered(1)),
                  mod_lat, mod_lat, mod_lat] + wspecs,
        out_specs=pl.BlockSpec((nt, bb, d), lambda ib, j, k: (j, ib, 0)),
        out_shape=jax.ShapeDtypeStruct((t_out, b, d), F32),
        scratch_shapes=[pltpu.VMEM((nt + 2 * w, bb, d), BF16),
                        pltpu.VMEM(((r + 2) * (w + 1) + 1, bb, ck), F32)],
        compiler_params=_cparams(("arbitrary", "arbitrary", "arbitrary")),
        name="conv_ffn_latent",
    )(xall, xall, xall, sc, sh, g2, *wargs)
    if not ctx_out:
        return out

    cblk = n_lat_t // tc
    mod_ctx = pl.BlockSpec((1, bb, d), lambda ib, j, k: (1, ib, 0))
    return pl.pallas_call(
        functools.partial(_ffn_kernel, piece=min(w, tc), vertical=False, n_row_blocks=1, nk=nk, alpha=alpha),
        grid=(b // bb, 1, nk),
        in_specs=[pl.BlockSpec(memory_space=pl.ANY),
                  pl.BlockSpec((tc, bb, d), lambda ib, j, k: (cblk, ib, 0), pipeline_mode=pl.Buffered(1)),
                  mod_ctx, mod_ctx, mod_ctx] + wspecs,
        out_specs=pl.BlockSpec((tc, bb, d), lambda ib, j, k: (cblk, ib, 0)),
        out_shape=jax.ShapeDtypeStruct((ttot, b, d), F32),
        scratch_shapes=[pltpu.VMEM((tc, bb, d), BF16),
                        pltpu.VMEM((tc + 2, bb, ck), F32)],
        input_output_aliases={0: 0},
        compiler_params=_cparams(("arbitrary", "arbitrary", "arbitrary")),
        name="conv_ffn_context",
    )(out, xall, sc, sh, g2, *wargs)


def _pack_gate_weights(wa, wx):
    h, hd = wa.shape[-3], wa.shape[-1]
    d = h * hd
    tile = min(MXU_TILE, d)
    hpt = tile // hd
    lead = wa.shape[:-3]
    eye = jnp.eye(hpt, dtype=wa.dtype)

    def dense(w):
        w = w.reshape(lead + (d // tile, hpt, hd, hd))
        return jnp.einsum('...qhij,hg->...qhigj', w, eye).reshape(lead + (d // tile, tile, tile))

    return jnp.concatenate([dense(wa), dense(wx)], axis=-1).astype(BF16)


def _pack_s5(ar, ai, bbr, bbi, c_re, c_im):
    lead = bbr.shape[:-3]
    g, p, s = bbr.shape[-3:]
    gpt = min(MXU_TILE // s, g)
    nh = g // gpt
    eye = jnp.eye(gpt, dtype=F32)

    def b_dense(bb_):
        bb_ = bb_.reshape(lead + (nh, gpt, p, s))
        return jnp.einsum('...hgpc,gk->...hgckp', bb_, eye).reshape(lead + (nh, gpt * s, gpt * p))

    def c_dense(cc):
        cc = cc.reshape(lead + (nh, gpt, s, p))
        return jnp.einsum('...hgcp,gk->...hgpkc', cc, eye).reshape(lead + (nh, gpt * p, gpt * s))

    ns = gpt * p
    cw = min(S5_CHUNK, ns)
    nchunk = ns // cw
    b_r = b_dense(bbr).reshape(lead + (nh, gpt * s, nchunk, 1, cw))
    b_i = b_dense(bbi).reshape(lead + (nh, gpt * s, nchunk, 1, cw))
    bcat = jnp.concatenate([b_r, b_i], axis=-2).reshape(lead + (nh, gpt * s, 2 * ns)).astype(BF16)
    c_r = c_dense(c_re).reshape(lead + (nh, nchunk, 1, cw, gpt * s))
    c_i = c_dense(-c_im).reshape(lead + (nh, nchunk, 1, cw, gpt * s))
    ccat = jnp.concatenate([c_r, c_i], axis=-3).reshape(lead + (nh, 2 * ns, gpt * s)).astype(BF16)
    are = ar.reshape(lead + (nh, 1, gpt * p))
    aim = ai.reshape(lead + (nh, 1, gpt * p))
    return bcat, are, aim, ccat


def kernel(x, c, ctx, c_ctx, ada_w, ada_b, w_in, lru_conv_w, lru_conv_b, lru_wa, lru_ba, lru_wx, lru_bx, lru_lam, s5_lam_re, s5_lam_im, s5_log_dt, s5_b_re, s5_b_im, s5_c_re, s5_c_im, s5_d, s5_glu_w, s5_glu_b, p_a, p_b, w_out, ln1_g, ln1_b, ffn_up, ffn_conv_w, ffn_conv_b, ffn_down, ln2_g, ln2_b):
    bsz, seq, d = x.shape
    n_ctx = ctx.shape[1]
    depth = ada_w.shape[0]
    d_lru = lru_lam.shape[-1]
    d_s5 = s5_d.shape[-1]
    dff = ffn_down.shape[1]
    alpha = (2.0 * depth) ** 0.25
    assert seq % n_ctx == 0 and seq % GRID_W == 0 and n_ctx % TB_MIX == 0 and n_ctx % TB_LAYOUT == 0

    m_rows = -(-(bsz + 1) // 8) * 8
    cond = jnp.zeros((m_rows, d), F32).at[:bsz].set(c).at[bsz].set(c_ctx)
    modv = _modulation(cond, ada_w, ada_b).reshape(depth, m_rows, 6, d)

    ar, ai, bbr, bbi = _s5_discretise(s5_lam_re, s5_lam_im, s5_log_dt, s5_b_re, s5_b_im)
    bcat, are, aim, ccat = _pack_s5(ar, ai, bbr, bbi, s5_c_re, s5_c_im)
    sp = _softplus_neg(lru_lam.reshape(depth * 2, d_lru)).reshape(depth, 2, 1, d_lru)
    wg = _pack_gate_weights(lru_wa, lru_wx)
    w_in16 = w_in.astype(BF16)
    row = lambda a: a[..., None, :]

    xall = _to_time_major(x, ctx)
    for l in range(depth):
        ctx_out = l < depth - 1
        lat = modv[l, :bsz]
        cx = jnp.broadcast_to(modv[l, bsz][None], (bsz, 6, d))
        mods = jnp.stack([lat, cx], axis=0)
        sh1, sc1, g1, sh2, sc2, g2 = (mods[:, :, n] for n in range(6))
        lp = dict(conv_w=lru_conv_w[l][:, :, None, :], conv_b=row(lru_conv_b[l]), wg=wg[l], ba=row(lru_ba[l]),
                  bx=row(lru_bx[l]), sp=sp[l], bcat=bcat[l], are=are[l], aim=aim[l], ccat=ccat[l],
                  d_skip=row(s5_d[l]), glu_w=s5_glu_w[l].astype(BF16), glu_b=row(s5_glu_b[l]),
                  p_a=p_a[l].astype(BF16), p_b=p_b[l].astype(BF16), w_out=w_out[l].astype(BF16),
                  ln1_g=row(ln1_g[l]), ln1_b=row(ln1_b[l]),
                  ffn_up=ffn_up[l].astype(BF16), ffn_down=ffn_down[l].astype(BF16),
                  ffn_conv_w=ffn_conv_w[l].reshape(9, 1, dff), ffn_conv_b=row(ffn_conv_b[l]),
                  ln2_g=row(ln2_g[l]), ln2_b=row(ln2_b[l]))
        wl = w_in16[l]
        w_parts = (wl[:, :d_lru], wl[:, d_lru:d_lru + d_s5], wl[:, d_lru + d_s5:d_lru + d_s5 + d],
                   wl[:, d_lru + d_s5 + d:])
        xmid = _mixer(xall, sc1, sh1, g1, w_parts, lp, seq, ctx_out, alpha)
        xall = _conv_ffn(xmid, sc2, sh2, g2, lp, seq, ctx_out, alpha)
    return _to_batch_major(xall)
```

```python
import functools
import math

import jax
import jax.numpy as jnp
from jax import lax
from jax.experimental import pallas as pl
from jax.experimental.pallas import tpu as pltpu

F32 = jnp.float32
BF16 = jnp.bfloat16

GRID_W = 64
LRU_C = 8.0
LN_EPS = 1e-5
MXU_TILE = 256
SUBLANES = 8
VMEM_LIMIT = 60 * 1024 * 1024
MOD_TN = 1536

TB_LAYOUT = 32
TB_MIX = 16
FFN_ROWS = 4
FFN_BB = 8
FFN_CK = 512
LRU_CHUNK = 512
S5_CHUNK = 256


def _cparams(sem):
    return pltpu.CompilerParams(dimension_semantics=sem, vmem_limit_bytes=VMEM_LIMIT)


def _const_spec(shape):
    nd = len(shape)
    return pl.BlockSpec(shape, lambda *_: (0,) * nd, pipeline_mode=pl.Buffered(1))


def _gelu_tanh(x):
    return 0.5 * x * (1.0 + jnp.tanh(math.sqrt(2.0 / math.pi) * (x + 0.044715 * (x * x * x))))


def _sigmoid(x):
    return 1.0 / (1.0 + jnp.exp(-x))


def _layer_norm(v, g, b):
    mu = jnp.mean(v, axis=-1, keepdims=True)
    xc = v - mu
    var = jnp.mean(xc * xc, axis=-1, keepdims=True)
    return xc * lax.rsqrt(var + LN_EPS) * g + b


def _to_time_major_kernel(x_ref, c_ref, o_ref, *, nl):
    i = pl.program_id(0)

    @pl.when(i < nl)
    def _():
        o_ref[...] = pltpu.einshape("btd->tbd", x_ref[...])

    @pl.when(i >= nl)
    def _():
        o_ref[...] = pltpu.einshape("btd->tbd", c_ref[...])


def _to_time_major(x, ctx):
    b, t, d = x.shape
    tc = ctx.shape[1]
    tb = TB_LAYOUT
    nl = t // tb
    return pl.pallas_call(
        functools.partial(_to_time_major_kernel, nl=nl),
        grid=((t + tc) // tb,),
        in_specs=[pl.BlockSpec((b, tb, d), lambda i: (0, jnp.minimum(i, nl - 1), 0)),
                  pl.BlockSpec((b, tb, d), lambda i: (0, jnp.maximum(i - nl, 0), 0))],
        out_specs=pl.BlockSpec((tb, b, d), lambda i: (i, 0, 0)),
        out_shape=jax.ShapeDtypeStruct((t + tc, b, d), x.dtype),
        compiler_params=_cparams(("arbitrary",)),
        name="to_time_major",
    )(x, ctx)


def _to_batch_major_kernel(x_ref, o_ref):
    o_ref[...] = pltpu.einshape("tbd->btd", x_ref[...])


def _to_batch_major(xt):
    t, b, d = xt.shape
    tb = TB_LAYOUT
    return pl.pallas_call(
        _to_batch_major_kernel,
        grid=(t // tb,),
        in_specs=[pl.BlockSpec((tb, b, d), lambda i: (i, 0, 0))],
        out_specs=pl.BlockSpec((b, tb, d), lambda i: (0, i, 0)),
        out_shape=jax.ShapeDtypeStruct((b, t, d), xt.dtype),
        compiler_params=_cparams(("arbitrary",)),
        name="to_batch_major",
    )(xt)


def _mod_kernel(c_ref, w_ref, b_ref, o_ref):
    c = c_ref[...]
    s = (c * _sigmoid(c)).astype(BF16)
    o_ref[0] = jnp.dot(s, w_ref[0].astype(BF16), preferred_element_type=F32) + b_ref[0]


def _modulation(cond, ada_w, ada_b):
    depth, d, n = ada_w.shape
    m = cond.shape[0]
    tn = min(n, MOD_TN)
    return pl.pallas_call(
        _mod_kernel,
        grid=(depth, n // tn),
        in_specs=[pl.BlockSpec((m, d), lambda l, j: (0, 0)),
                  pl.BlockSpec((1, d, tn), lambda l, j: (l, 0, j)),
                  pl.BlockSpec((1, 1, tn), lambda l, j: (l, 0, j))],
        out_specs=pl.BlockSpec((1, m, tn), lambda l, j: (l, 0, j)),
        out_shape=jax.ShapeDtypeStruct((depth, m, n), F32),
        compiler_params=_cparams(("arbitrary", "arbitrary")),
        name="adaln_mod",
    )(cond, ada_w, ada_b.reshape(depth, 1, n))


def _s5_disc_kernel(lr_ref, li_ref, ldt_ref, br_ref, bi_ref, ar_ref, ai_ref, bbr_ref, bbi_ref):
    lr, li = lr_ref[...], li_ref[...]
    dt = jnp.exp(ldt_ref[...])
    mag = jnp.exp(lr * dt)
    ar = mag * jnp.cos(li * dt)
    ai = mag * jnp.sin(li * dt)
    den = lr * lr + li * li
    fr = ((ar - 1.0) * lr + ai * li) / den
    fi = (ai * lr - (ar - 1.0) * li) / den
    br, bi = br_ref[...], bi_ref[...]
    ar_ref[...] = ar
    ai_ref[...] = ai
    bbr_ref[...] = fr * br - fi * bi
    bbi_ref[...] = fr * bi + fi * br


def _s5_discretise(lam_re, lam_im, log_dt, b_re, b_im):
    lead = b_re.shape[:-2]
    p, s = b_re.shape[-2:]
    n = math.prod(lead)
    full = lead + (p, s)
    flat = lambda a: jnp.broadcast_to(a, full).reshape(n, p * s)
    args = (flat(lam_re[..., None]), flat(lam_im[..., None]), flat(log_dt[..., None, None]), flat(b_re), flat(b_im))
    rb = SUBLANES if n % SUBLANES == 0 else n
    spec = pl.BlockSpec((rb, p * s), lambda i: (i, 0))
    outs = pl.pallas_call(
        _s5_disc_kernel,
        grid=(n // rb,),
        in_specs=[spec] * 5,
        out_specs=[spec] * 4,
        out_shape=[jax.ShapeDtypeStruct((n, p * s), F32)] * 4,
        compiler_params=_cparams(("arbitrary",)),
        name="s5_discretise",
    )(*args)
    ar, ai, bbr, bbi = (o.reshape(full) for o in outs)
    return ar[..., 0], ai[..., 0], bbr, bbi


def _softplus_neg_kernel(lam_ref, o_ref):
    z = -lam_ref[...]
    o_ref[...] = LRU_C * (jnp.maximum(z, 0.0) + jnp.log(1.0 + jnp.exp(-jnp.abs(z))))


def _softplus_neg(lam2d):
    return pl.pallas_call(
        _softplus_neg_kernel,
        out_shape=jax.ShapeDtypeStruct(lam2d.shape, F32),
        name="lru_softplus",
    )(lam2d)


def _mixer_direction(reverse, seq_start, any_start, xa_ref, xb_ref, cw_ref, cb_ref, wg_ref, ba_ref, bx_ref,
                     sp_ref, bcat_ref, are_ref, aim_ref, ccat_ref, ext_s, a_s, g_s, hl_s, hb_s, hs_s, ys_s,
                     inproj=None):
    tb, bb, d = xa_ref.shape
    rows = tb * bb
    k_w = cw_ref.shape[0]
    halo = k_w - 1
    ds = xb_ref.shape[2]
    n_half = bcat_ref.shape[0]
    hw = ds // n_half
    ns = are_ref.shape[2]
    cw = hb_s.shape[3] // 2
    nt = wg_ref.shape[0]
    tw = d // nt
    x_lo = 0 if reverse else halo
    h_new, h_old = (tb, 0) if reverse else (0, tb)
    order = lambda s: tb - 1 - s if reverse else s

    @pl.when(any_start)
    def _():
        hl_s[...] = jnp.zeros_like(hl_s)
        hs_s[...] = jnp.zeros_like(hs_s)

    @pl.when(seq_start)
    def _():
        ext_s[h_new:h_new + halo] = jnp.zeros((halo, bb, d), F32)

    @pl.when(jnp.logical_not(seq_start))
    def _():
        ext_s[h_new:h_new + halo] = ext_s[h_old:h_old + halo]

    fillers = []
    u_half = {}
    if inproj is None:
        ext_s[x_lo:x_lo + tb] = xa_ref[...]
        u = xb_ref[...].reshape(rows, ds)
        for hf in range(n_half):
            u_half[hf] = u[:, hf * hw:(hf + 1) * hw]
    else:
        x_ref, sc_ref, sh_ref, wa_ref, wb_ref, wga_ref, wgb_ref, ga_ref, gb_ref = inproj
        xm = (x_ref[...] * (1.0 + sc_ref[0]) + sh_ref[0]).reshape(rows, d).astype(BF16)

        def project_xb(hf):
            v = jnp.dot(xm, wb_ref[:, hf * hw:(hf + 1) * hw], preferred_element_type=F32).astype(BF16)
            xb_ref[:, :, hf * hw:(hf + 1) * hw] = v.reshape(tb, bb, hw)
            u_half[hf] = v

        def project_xa(q):
            sl = slice(q * tw, (q + 1) * tw)
            v = jnp.dot(xm, wa_ref[:, sl], preferred_element_type=F32).reshape(tb, bb, tw)
            xa_ref[:, :, sl] = v
            ext_s[x_lo:x_lo + tb, :, sl] = v

        def project_gate(w_ref, o_ref, q):
            sl = slice(q * tw, (q + 1) * tw)
            v = jnp.dot(xm, w_ref[:, sl], preferred_element_type=F32)
            o_ref[:, :, sl] = v.reshape(tb, bb, tw).astype(o_ref.dtype)

        project_xb(0)
        project_xa(0)
        fillers = ([functools.partial(project_xb, hf) for hf in range(1, n_half)]
                   + [functools.partial(project_xa, q) for q in range(1, nt)]
                   + [functools.partial(project_gate, w, o, q)
                      for w, o in ((wga_ref, ga_ref), (wgb_ref, gb_ref)) for q in range(nt)])

    def s5_project(hf, c):
        cols = slice(c * 2 * cw, (c + 1) * 2 * cw)
        bu = jnp.dot(u_half[hf], bcat_ref[hf, :, cols], preferred_element_type=F32)
        return bu.reshape(tb, bb, 2 * cw)

    def s5_scan(hf, c, bu, slot):
        lo = c * 2 * cw
        ar = jnp.broadcast_to(are_ref[hf, :, c * cw:(c + 1) * cw], (bb, cw))
        ai = jnp.broadcast_to(aim_ref[hf, :, c * cw:(c + 1) * cw], (bb, cw))
        hr, hi = hs_s[hf, :, lo:lo + cw], hs_s[hf, :, lo + cw:lo + 2 * cw]
        for s in range(tb):
            t = order(s)
            hr, hi = ar * hr - ai * hi + bu[t, :, :cw], ar * hi + ai * hr + bu[t, :, cw:]
            hb_s[slot, t, :, :cw] = hr.astype(BF16)
            hb_s[slot, t, :, cw:] = hi.astype(BF16)
        hs_s[hf, :, lo:lo + cw] = hr
        hs_s[hf, :, lo + cw:lo + 2 * cw] = hi

    def s5_readout(hf, c, slot):
        cols = slice(c * 2 * cw, (c + 1) * 2 * cw)
        part = jnp.dot(hb_s[slot].reshape(rows, 2 * cw), ccat_ref[hf, cols, :], preferred_element_type=F32)
        if c == 0:
            ys_s[:, hf * hw:(hf + 1) * hw] = part
        else:
            ys_s[:, hf * hw:(hf + 1) * hw] += part

    def gate_matmul(q):
        sl = slice(q * tw, (q + 1) * tw)
        xq = cb_ref[:, sl] + cw_ref[k_w - 1:k_w, :, sl] * ext_s[x_lo:x_lo + tb, :, sl]
        for k in range(k_w - 1):
            off = halo - k if reverse else k
            xq = xq + cw_ref[k:k + 1, :, sl] * ext_s[off:off + tb, :, sl]
        xq = xq.reshape(rows, tw)
        return q, xq, jnp.dot(xq.astype(BF16), wg_ref[q], preferred_element_type=F32)

    def gate_coefficients(q, xq, z):
        sl = slice(q * tw, (q + 1) * tw)
        r = _sigmoid(z[:, :tw] + ba_ref[:, sl])
        gi = _sigmoid(z[:, tw:] + bx_ref[:, sl])
        nla = r * sp_ref[:, sl]
        a = jnp.exp(-nla)
        gx = jnp.sqrt(jnp.tanh(nla) * (a * a + 1.0)) * (gi * xq)
        a_s[:, :, sl] = a.reshape(tb, bb, tw)
        g_s[:, :, sl] = gx.reshape(tb, bb, tw)

    pieces = [(hf, c) for hf in range(n_half) for c in range(ns // cw)]
    tiles_started = 0
    bu_next = s5_project(*pieces[0])
    pending = None
    per_piece = -(-len(fillers) // max(1, len(pieces) - 1))
    for n, (hf, c) in enumerate(pieces):
        bu = bu_next
        for _ in range(min(per_piece, len(fillers))):
            fillers.pop(0)()
        if n + 1 < len(pieces):
            bu_next = s5_project(*pieces[n + 1])
        started = None
        if tiles_started < nt and n * nt >= tiles_started * len(pieces):
            started = gate_matmul(tiles_started)
            tiles_started += 1
        s5_scan(hf, c, bu, n % 2)
        s5_readout(hf, c, n % 2)
        if pending is not None:
            gate_coefficients(*pending)
        pending = started
    if pending is not None:
        gate_coefficients(*pending)
    while fillers:
        fillers.pop(0)()
    for q in range(tiles_started, nt):
        gate_coefficients(*gate_matmul(q))

    cl = min(LRU_CHUNK, d)
    for c0 in range(0, d, cl):
        cs = slice(c0, c0 + cl)
        h = hl_s[:, cs]
        for s in range(tb):
            t = order(s)
            h = a_s[t, :, cs] * h + g_s[t, :, cs]
            g_s[t, :, cs] = h
        hl_s[:, cs] = h


def _mixer_rev_kernel(x_ref, sc_ref, sh_ref, wa_ref, wb_ref, wga_ref, wgb_ref, cw_ref, cb_ref, wg_ref, ba_ref,
                      bx_ref, sp_ref, bcat_ref, are_ref, aim_ref, ccat_ref, ybl_ref, ybs_ref, xa_ref, xb_ref, ga_ref,
                      gb_ref, ext_s, a_s, g_s, hl_s, hb_s, hs_s, ys_s, *, nc):
    i = pl.program_id(0)
    tb, bb, _ = xa_ref.shape
    _mixer_direction(True, jnp.logical_or(i == 0, i == nc), i == 0, xa_ref, xb_ref, cw_ref, cb_ref, wg_ref,
                     ba_ref, bx_ref, sp_ref, bcat_ref, are_ref, aim_ref, ccat_ref, ext_s, a_s, g_s, hl_s, hb_s,
                     hs_s, ys_s,
                     inproj=(x_ref, sc_ref, sh_ref, wa_ref, wb_ref, wga_ref, wgb_ref, ga_ref, gb_ref))
    ybl_ref[...] = g_s[...].astype(ybl_ref.dtype)
    ybs_ref[...] = ys_s[...].reshape(tb, bb, ys_s.shape[-1]).astype(ybs_ref.dtype)


def _mixer_fwd_kernel(xa_ref, xb_ref, cw_ref, cb_ref, wg_ref, ba_ref, bx_ref, sp_ref, bcat_ref, are_ref,
                      aim_ref, ccat_ref, ybl_ref, ybs_ref, ga_ref, gb_ref, x_ref, g1_ref, dsk_ref, gluw_ref,
                      glub_ref, pa_ref, pb_ref, wo_ref, lng_ref, lnb_ref, o_ref,
                      ext_s, a_s, g_s, hl_s, hb_s, hs_s, ys_s, *, nc, ctx_out, alpha):
    i = pl.program_id(0)
    tb, bb, d = xa_ref.shape
    rows = tb * bb
    _mixer_direction(False, jnp.logical_or(i == 0, i == nc), i == 0, xa_ref, xb_ref, cw_ref, cb_ref, wg_ref,
                     ba_ref, bx_ref, sp_ref, bcat_ref, are_ref, aim_ref, ccat_ref, ext_s, a_s, g_s, hl_s, hb_s,
                     hs_s, ys_s)

    def merge():
        ds = xb_ref.shape[2]
        ya = (g_s[...] + ybl_ref[...].astype(F32)).reshape(rows, d).astype(BF16)
        ta = jnp.dot(ya, pa_ref[...], preferred_element_type=F32)
        u = xb_ref[...].reshape(rows, ds).astype(F32)
        yb = ys_s[...] + ybs_ref[...].reshape(rows, ds).astype(F32) + dsk_ref[...] * u
        z = _gelu_tanh(yb)
        zg = jnp.dot(z.astype(BF16), gluw_ref[...], preferred_element_type=F32) + glub_ref[...]
        ma = _sigmoid(ga_ref[...].reshape(rows, d).astype(F32)) * ta
        yb = (z * _sigmoid(zg)).astype(BF16)
        tbp = jnp.dot(yb, pb_ref[...], preferred_element_type=F32)
        m = ma + _sigmoid(gb_ref[...].reshape(rows, d).astype(F32)) * tbp
        out = jnp.dot(m.astype(BF16), wo_ref[...], preferred_element_type=F32).reshape(tb, bb, d)
        v = alpha * x_ref[...] + (1.0 + g1_ref[0]) * out
        o_ref[...] = _layer_norm(v, lng_ref[...], lnb_ref[...])

    if ctx_out:
        merge()
    else:
        pl.when(i >= nc)(merge)


def _mixer(xall, sc1, sh1, g1, w_parts, lp, n_lat_t, ctx_out, alpha):
    ttot, b, d = xall.shape
    w_a, w_b, w_ga, w_gb = w_parts
    ds = w_b.shape[1]
    tb = TB_MIX
    nblk = ttot // tb
    nl = n_lat_t // tb
    nc = nblk - nl
    ns2 = lp["bcat"].shape[-1]
    n_half = lp["bcat"].shape[1]

    scratch = [pltpu.VMEM((tb + lp["conv_w"].shape[1] - 1, b, d), F32),
               pltpu.VMEM((tb, b, d), F32),
               pltpu.VMEM((tb, b, d), F32),
               pltpu.VMEM((b, d), F32),
               pltpu.VMEM((2, tb, b, 2 * min(S5_CHUNK, ns2 // 2)), BF16),
               pltpu.VMEM((n_half, b, ns2), F32),
               pltpu.VMEM((tb * b, ds), F32)]

    def dir_args(dr):
        return (lp["conv_w"][dr], lp["conv_b"][dr], lp["wg"][dr], lp["ba"][dr], lp["bx"][dr], lp["sp"][dr],
                lp["bcat"][dr], lp["are"][dr], lp["aim"][dr], lp["ccat"][dr])

    def dir_specs(args):
        return [_const_spec(a.shape) for a in args]

    rev = lambda i: (nblk - 1 - i, 0, 0)
    tok_rev = lambda c: pl.BlockSpec((tb, b, c), rev)
    mod_rev = pl.BlockSpec((1, b, d), lambda i: (jnp.where(nblk - 1 - i >= nl, 1, 0), 0, 0))
    tok_shape = lambda c, dt: jax.ShapeDtypeStruct((ttot, b, c), dt)
    args = (w_a, w_b, w_ga, w_gb) + dir_args(1)
    ybl, ybs, xa, xb, ga, gb = pl.pallas_call(
        functools.partial(_mixer_rev_kernel, nc=nc),
        grid=(nblk,),
        in_specs=[tok_rev(d), mod_rev, mod_rev] + dir_specs(args),
        out_specs=[tok_rev(d), tok_rev(ds), tok_rev(d), tok_rev(ds), tok_rev(d), tok_rev(d)],
        out_shape=[tok_shape(d, BF16), tok_shape(ds, BF16), tok_shape(d, F32), tok_shape(ds, BF16),
                   tok_shape(d, BF16), tok_shape(d, BF16)],
        scratch_shapes=scratch,
        compiler_params=_cparams(("arbitrary",)),
        name="mixer_rev",
    )(xall, sc1, sh1, *args)

    fwd = lambda i: (jnp.where(i < nc, nl + i, i - nc), 0, 0)
    tok_fwd = lambda c: pl.BlockSpec((tb, b, c), fwd)
    if ctx_out:
        out_spec, t_out = tok_fwd(d), ttot
    else:
        out_spec, t_out = pl.BlockSpec((tb, b, d), lambda i: (jnp.maximum(i - nc, 0), 0, 0)), n_lat_t
    g1_spec = pl.BlockSpec((1, b, d), lambda i: (jnp.where(i < nc, 1, 0), 0, 0))
    args = dir_args(0)
    tail = (lp["d_skip"], lp["glu_w"], lp["glu_b"], lp["p_a"], lp["p_b"], lp["w_out"], lp["ln1_g"], lp["ln1_b"])
    return pl.pallas_call(
        functools.partial(_mixer_fwd_kernel, nc=nc, ctx_out=ctx_out, alpha=alpha),
        grid=(nblk,),
        in_specs=([tok_fwd(d), tok_fwd(ds)] + dir_specs(args)
                  + [tok_fwd(d), tok_fwd(ds), tok_fwd(d), tok_fwd(d), tok_fwd(d), g1_spec] + dir_specs(tail)),
        out_specs=out_spec,
        out_shape=jax.ShapeDtypeStruct((t_out, b, d), F32),
        scratch_shapes=scratch,
        compiler_params=_cparams(("arbitrary",)),
        name="mixer_fwd",
    )(xa, xb, *args, ybl, ybs, ga, gb, xall, g1, *tail)


def _ffn_kernel(*refs, piece, vertical, n_row_blocks, nk, alpha):
    if vertical:
        (x_ref, xt_ref, xbm_ref, sc_ref, sh_ref, g2_ref, wg_ref, wv_ref, wd_ref, cw_ref, cb_ref, lng_ref, lnb_ref,
         o_ref, u_s, gs_s) = refs[-16:]
    else:
        (x_ref, sc_ref, sh_ref, g2_ref, wg_ref, wv_ref, wd_ref, cw_ref, cb_ref, lng_ref, lnb_ref,
         o_ref, u_s, gs_s) = refs[-14:]
    j, k = pl.program_id(1), pl.program_id(2)
    nt, bb, d = x_ref.shape
    ck = wg_ref.shape[1]
    n_pre = 1 if vertical else 0
    n_main = nt // piece
    n_ext = n_main + 2 * n_pre
    vo = n_pre * piece
    stride = piece + 1 if vertical else piece

    mod = lambda v: (v * (1.0 + sc_ref[0]) + sh_ref[0]).astype(BF16)
    drs = (-1, 0, 1) if vertical else (0,)

    def fill_u(e):
        if vertical and e == 0:
            src = xt_ref[...]
        elif vertical and e == n_ext - 1:
            src = xbm_ref[...]
        else:
            src = x_ref[(e - n_pre) * piece:(e - n_pre + 1) * piece]
        u_s[e * piece:(e + 1) * piece] = mod(src)

    def gate_piece(e):
        g = jnp.dot(u_s[e * piece:(e + 1) * piece].reshape(piece * bb, d), wg_ref[...],
                    preferred_element_type=F32)
        if vertical and e == 0:
            g = jnp.where(j == 0, 0.0, g)
        if vertical and e == n_ext - 1:
            g = jnp.where(j == n_row_blocks - 1, 0.0, g)
        gs_s[1 + e * stride:1 + e * stride + piece] = g.reshape(piece, bb, ck)

    def body(first, last):
        for p in (range(n_ext + 1) if vertical else (0,)):
            gs_s[p * stride] = jnp.zeros((bb, ck), F32)
        if not vertical:
            gs_s[1 + nt] = jnp.zeros((bb, ck), F32)
        gates_done = 0
        for r in range(n_main):
            need = min(r + n_pre + 1, n_ext - 1)
            while gates_done <= need:
                if first:
                    fill_u(gates_done)
                gate_piece(gates_done)
                gates_done += 1
            rows_r = slice(r * piece, (r + 1) * piece)
            val = jnp.dot(u_s[vo + r * piece:vo + (r + 1) * piece].reshape(piece * bb, d), wv_ref[...],
                          preferred_element_type=F32)
            conv = None
            for dr in drs:
                for dc in (-1, 0, 1):
                    s0 = 1 + (r + n_pre + dr) * stride + dc
                    tap = (dr + 1) * 3 + dc + 1
                    term = cw_ref[tap:tap + 1] * gs_s[s0:s0 + piece]
                    conv = term if conv is None else conv + term
            act = _gelu_tanh(conv.reshape(piece * bb, ck) + cb_ref[...]) * val
            part = jnp.dot(act.astype(BF16), wd_ref[...], preferred_element_type=F32).reshape(piece, bb, d)
            acc = part if first else o_ref[rows_r] + part
            if last:
                v = alpha * x_ref[rows_r] + (1.0 + g2_ref[0]) * acc
                acc = _layer_norm(v, lng_ref[...], lnb_ref[...])
            o_ref[rows_r] = acc

    if nk == 1:
        body(True, True)
    else:
        pl.when(k == 0)(functools.partial(body, True, False))
        if nk > 2:
            pl.when(jnp.logical_and(k > 0, k < nk - 1))(functools.partial(body, False, False))
        pl.when(k == nk - 1)(functools.partial(body, False, True))


def _conv_ffn(xall, sc, sh, g2, lp, n_lat_t, ctx_out, alpha):
    ttot, b, d = xall.shape
    dff = lp["ffn_down"].shape[0]
    ck = min(FFN_CK, dff)
    nk = dff // ck
    bb = min(FFN_BB, b)
    tc = ttot - n_lat_t
    w_up, w_down = lp["ffn_up"], lp["ffn_down"]

    wspecs = [pl.BlockSpec((d, ck), lambda ib, j, k: (0, k)),
              pl.BlockSpec((d, ck), lambda ib, j, k: (0, nk + k)),
              pl.BlockSpec((ck, d), lambda ib, j, k: (k, 0)),
              pl.BlockSpec((9, 1, ck), lambda ib, j, k: (0, 0, k)),
              pl.BlockSpec((1, ck), lambda ib, j, k: (0, k)),
              pl.BlockSpec((1, d), lambda ib, j, k: (0, 0)),
              pl.BlockSpec((1, d), lambda ib, j, k: (0, 0))]
    wargs = (w_up, w_up, w_down, lp["ffn_conv_w"], lp["ffn_conv_b"], lp["ln2_g"], lp["ln2_b"])

    w = GRID_W
    n_img_rows = n_lat_t // w
    r = min(FFN_ROWS, n_img_rows)
    nt = r * w
    nrb = n_img_rows // r
    mod_lat = pl.BlockSpec((1, bb, d), lambda ib, j, k: (0, ib, 0))
    t_out = ttot if ctx_out else n_lat_t
    out = pl.pallas_call(
        functools.partial(_ffn_kernel, piece=w, vertical=True, n_row_blocks=nrb, nk=nk, alpha=alpha),
        grid=(b // bb, nrb, nk),
        in_specs=[pl.BlockSpec((nt, bb, d), lambda ib, j, k: (j, ib, 0)),
                  pl.BlockSpec((w, bb, d), lambda ib, j, k: (jnp.maximum(j * r - 1, 0), ib, 0),
                               pipeline_mode=pl.Buffered(1)),
                  pl.BlockSpec((w, bb, d), lambda ib, j, k: (jnp.minimum(j * r + r, n_img_rows - 1), ib, 0),
                               pipeline_mode=pl.Buffered(1)),
                  mod_lat, mod_lat, mod_lat] + wspecs,
        out_specs=pl.BlockSpec((nt, bb, d), lambda ib, j, k: (j, ib, 0)),
        out_shape=jax.ShapeDtypeStruct((t_out, b, d), F32),
        scratch_shapes=[pltpu.VMEM((nt + 2 * w, bb, d), BF16),
                        pltpu.VMEM(((r + 2) * (w + 1) + 1, bb, ck), F32)],
        compiler_params=_cparams(("arbitrary", "arbitrary", "arbitrary")),
        name="conv_ffn_latent",
    )(xall, xall, xall, sc, sh, g2, *wargs)
    if not ctx_out:
        return out

    cblk = n_lat_t // tc
    mod_ctx = pl.BlockSpec((1, bb, d), lambda ib, j, k: (1, ib, 0))
    return pl.pallas_call(
        functools.partial(_ffn_kernel, piece=min(w, tc), vertical=False, n_row_blocks=1, nk=nk, alpha=alpha),
        grid=(b // bb, 1, nk),
        in_specs=[pl.BlockSpec(memory_space=pl.ANY),
                  pl.BlockSpec((tc, bb, d), lambda ib, j, k: (cblk, ib, 0), pipeline_mode=pl.Buffered(1)),
                  mod_ctx, mod_ctx, mod_ctx] + wspecs,
        out_specs=pl.BlockSpec((tc, bb, d), lambda ib, j, k: (cblk, ib, 0)),
        out_shape=jax.ShapeDtypeStruct((ttot, b, d), F32),
        scratch_shapes=[pltpu.VMEM((tc, bb, d), BF16),
                        pltpu.VMEM((tc + 2, bb, ck), F32)],
        input_output_aliases={0: 0},
        compiler_params=_cparams(("arbitrary", "arbitrary", "arbitrary")),
        name="conv_ffn_context",
    )(out, xall, sc, sh, g2, *wargs)


def _pack_gate_weights(wa, wx):
    h, hd = wa.shape[-3], wa.shape[-1]
    d = h * hd
    tile = min(MXU_TILE, d)
    hpt = tile // hd
    lead = wa.shape[:-3]
    eye = jnp.eye(hpt, dtype=BF16)

    def dense(w):
        w = w.astype(BF16).reshape(lead + (d // tile, hpt, hd, hd))
        return jnp.einsum('...qhij,hg->...qhigj', w, eye).reshape(lead + (d // tile, tile, tile))

    return jnp.concatenate([dense(wa), dense(wx)], axis=-1)


def _pack_s5(ar, ai, bbr, bbi, c_re, c_im):
    lead = bbr.shape[:-3]
    g, p, s = bbr.shape[-3:]
    gpt = min(MXU_TILE // s, g)
    nh = g // gpt
    eye = jnp.eye(gpt, dtype=BF16)

    def b_dense(bb_):
        bb_ = bb_.astype(BF16).reshape(lead + (nh, gpt, p, s))
        return jnp.einsum('...hgpc,gk->...hgckp', bb_, eye).reshape(lead + (nh, gpt * s, gpt * p))

    def c_dense(cc):
        cc = cc.astype(BF16).reshape(lead + (nh, gpt, s, p))
        return jnp.einsum('...hgcp,gk->...hgpkc', cc, eye).reshape(lead + (nh, gpt * p, gpt * s))

    ns = gpt * p
    cw = min(S5_CHUNK, ns)
    nchunk = ns // cw
    b_r = b_dense(bbr).reshape(lead + (nh, gpt * s, nchunk, 1, cw))
    b_i = b_dense(bbi).reshape(lead + (nh, gpt * s, nchunk, 1, cw))
    bcat = jnp.concatenate([b_r, b_i], axis=-2).reshape(lead + (nh, gpt * s, 2 * ns))
    c_r = c_dense(c_re).reshape(lead + (nh, nchunk, 1, cw, gpt * s))
    c_i = c_dense(-c_im).reshape(lead + (nh, nchunk, 1, cw, gpt * s))
    ccat = jnp.concatenate([c_r, c_i], axis=-3).reshape(lead + (nh, 2 * ns, gpt * s))
    are = ar.reshape(lead + (nh, 1, gpt * p))
    aim = ai.reshape(lead + (nh, 1, gpt * p))
    return bcat, are, aim, ccat


def kernel(x, c, ctx, c_ctx, ada_w, ada_b, w_in, lru_conv_w, lru_conv_b, lru_wa, lru_ba, lru_wx, lru_bx, lru_lam, s5_lam_re, s5_lam_im, s5_log_dt, s5_b_re, s5_b_im, s5_c_re, s5_c_im, s5_d, s5_glu_w, s5_glu_b, p_a, p_b, w_out, ln1_g, ln1_b, ffn_up, ffn_conv_w, ffn_conv_b, ffn_down, ln2_g, ln2_b):
    bsz, seq, d = x.shape
    n_ctx = ctx.shape[1]
    depth = ada_w.shape[0]
    d_lru = lru_lam.shape[-1]
    d_s5 = s5_d.shape[-1]
    dff = ffn_down.shape[1]
    alpha = (2.0 * depth) ** 0.25
    assert seq % n_ctx == 0 and seq % GRID_W == 0 and n_ctx % TB_MIX == 0 and n_ctx % TB_LAYOUT == 0

    m_rows = -(-(bsz + 1) // SUBLANES) * SUBLANES
    cond = jnp.zeros((m_rows, d), F32).at[:bsz].set(c).at[bsz].set(c_ctx)
    modv = _modulation(cond, ada_w, ada_b).reshape(depth, m_rows, 6, d)

    ar, ai, bbr, bbi = _s5_discretise(s5_lam_re, s5_lam_im, s5_log_dt, s5_b_re, s5_b_im)
    bcat, are, aim, ccat = _pack_s5(ar, ai, bbr, bbi, s5_c_re, s5_c_im)
    sp = _softplus_neg(lru_lam.reshape(depth * 2, d_lru)).reshape(depth, 2, 1, d_lru)
    wg = _pack_gate_weights(lru_wa, lru_wx)
    w_in16 = w_in.astype(BF16)
    row = lambda a: a[..., None, :]

    xall = _to_time_major(x, ctx)
    for l in range(depth):
        ctx_out = l < depth - 1
        lat = modv[l, :bsz]
        cx = jnp.broadcast_to(modv[l, bsz][None], (bsz, 6, d))
        mods = jnp.stack([lat, cx], axis=0)
        sh1, sc1, g1, sh2, sc2, g2 = (mods[:, :, n] for n in range(6))
        lp = dict(conv_w=lru_conv_w[l][:, :, None, :], conv_b=row(lru_conv_b[l]), wg=wg[l], ba=row(lru_ba[l]),
                  bx=row(lru_bx[l]), sp=sp[l], bcat=bcat[l], are=are[l], aim=aim[l], ccat=ccat[l],
                  d_skip=row(s5_d[l]), glu_w=s5_glu_w[l].astype(BF16), glu_b=row(s5_glu_b[l]),
                  p_a=p_a[l].astype(BF16), p_b=p_b[l].astype(BF16), w_out=w_out[l].astype(BF16),
                  ln1_g=row(ln1_g[l]), ln1_b=row(ln1_b[l]),
                  ffn_up=ffn_up[l].astype(BF16), ffn_down=ffn_down[l].astype(BF16),
                  ffn_conv_w=ffn_conv_w[l].reshape(9, 1, dff), ffn_conv_b=row(ffn_conv_b[l]),
                  ln2_g=row(ln2_g[l]), ln2_b=row(ln2_b[l]))
        wl = w_in16[l]
        w_parts = (wl[:, :d_lru], wl[:, d_lru:d_lru + d_s5], wl[:, d_lru + d_s5:d_lru + d_s5 + d],
                   wl[:, d_lru + d_s5 + d:])
        xmid = _mixer(xall, sc1, sh1, g1, w_parts, lp, seq, ctx_out, alpha)
        xall = _conv_ffn(xmid, sc2, sh2, g2, lp, seq, ctx_out, alpha)
    return _to_batch_major(xall)
```
